```python
import math
import jax, jax.numpy as jnp
from jax import lax
import numpy as np

D_MODEL = 2048
BATCH = 8
SEQ = 4096
DEPTH = 4

GRID_W = 64
N_MIXERS = 2
HEAD_DIM = 128
A_HEADS = 16
A_KV_HEADS = 4
ROPE_THETA = 10000.0
A_Q_BLOCK = 128
A_QKV_WIDTH = (A_HEADS + 2 * A_KV_HEADS) * HEAD_DIM
B_GROUPS = ((128, 1), (512, 4), (2048, 16))
B_HEADS_PER_GROUP = 8
B_Q_BLOCK = 64
B_WIDTH = len(B_GROUPS) * B_HEADS_PER_GROUP * HEAD_DIM
B_QKV_WIDTH = 3 * B_WIDTH
REL_BUCKETS = 32
REL_MAX_DISTANCE = 1024
D_FF = 5632
CONV_WIDTH = 3
EPS = 1e-6
NEG_INF = -1e30
N_A_LAYERS = (DEPTH + 1) // 2
N_B_LAYERS = DEPTH // 2

kernel_name = "hybrid_axial_gqa_dilated_convffn_encoder"


def rms_norm(x, gain):
    xf = x.astype(jnp.float32)
    y = xf * lax.rsqrt(jnp.mean(xf * xf, axis=-1, keepdims=True) + EPS) * gain.astype(jnp.float32)
    return y.astype(x.dtype)


def axial_rope(x):
    seq = x.shape[1]
    rows = seq // GRID_W
    row_ids = jnp.repeat(jnp.arange(rows, dtype=jnp.float32), GRID_W)
    col_ids = jnp.tile(jnp.arange(GRID_W, dtype=jnp.float32), rows)
    half = HEAD_DIM // 2
    quarter = half // 2
    inv_freq = ROPE_THETA ** (-jnp.arange(quarter, dtype=jnp.float32) / quarter)

    def rot(xs, pos):
        ang = pos[:, None] * inv_freq[None, :]
        cos = jnp.cos(ang)[None, :, None, :]
        sin = jnp.sin(ang)[None, :, None, :]
        x1, x2 = xs[..., :quarter], xs[..., quarter:]
        return jnp.concatenate([x1 * cos - x2 * sin, x2 * cos + x1 * sin], axis=-1)

    return jnp.concatenate([rot(x[..., :half], row_ids), rot(x[..., half:], col_ids)], axis=-1)


def mixer_a(h, w_qkv, w_o, q_gain, k_gain):
    b, s, _ = h.shape
    qkv = h @ w_qkv
    nq = A_HEADS * HEAD_DIM
    nk = A_KV_HEADS * HEAD_DIM
    q = qkv[..., :nq].reshape(b, s, A_HEADS, HEAD_DIM)
    k = qkv[..., nq:nq + nk].reshape(b, s, A_KV_HEADS, HEAD_DIM)
    v = qkv[..., nq + nk:].reshape(b, s, A_KV_HEADS, HEAD_DIM)
    q = axial_rope(rms_norm(q.astype(jnp.float32), q_gain)).astype(h.dtype)
    k = axial_rope(rms_norm(k.astype(jnp.float32), k_gain)).astype(h.dtype)
    grp = A_HEADS // A_KV_HEADS
    nblk = s // A_Q_BLOCK
    qb = q.reshape(b, nblk, A_Q_BLOCK, A_KV_HEADS, grp, HEAD_DIM).transpose(1, 0, 2, 3, 4, 5)
    scale = HEAD_DIM ** -0.5

    def attend(q_blk):
        logits = jnp.einsum('bqkgd,bskd->bkgqs', q_blk, k).astype(jnp.float32) * scale
        p = jax.nn.softmax(logits, axis=-1).astype(v.dtype)
        return jnp.einsum('bkgqs,bskd->bqkgd', p, v)

    o = lax.map(attend, qb)
    o = o.transpose(1, 0, 2, 3, 4, 5).reshape(b, s, nq)
    return o @ w_o


def t5_bucket(rel):
    nb = REL_BUCKETS // 2
    max_exact = nb // 2
    base = jnp.where(rel > 0, nb, 0)
    n = jnp.abs(rel)
    nf = jnp.maximum(n, 1).astype(jnp.float32)
    large = max_exact + (jnp.log(nf / max_exact) / math.log(REL_MAX_DISTANCE / max_exact)
                         * (nb - max_exact)).astype(jnp.int32)
    large = jnp.minimum(large, nb - 1)
    return base + jnp.where(n < max_exact, n, large)


def dilated_group(q, k, v, rel_bias_g, window, dilation):
    b, s, h, d = q.shape
    half_span = window // (2 * dilation)
    L = s // dilation
    nblk = -(-L // B_Q_BLOCK)
    Lp = nblk * B_Q_BLOCK
    kv_len = B_Q_BLOCK + 2 * half_span
    qs = q.reshape(b, L, dilation, h, d)
    ks = k.reshape(b, L, dilation, h, d)
    vs = v.reshape(b, L, dilation, h, d)
    qs = jnp.pad(qs, ((0, 0), (0, Lp - L), (0, 0), (0, 0), (0, 0)))
    pad_kv = ((0, 0), (half_span, Lp - L + half_span), (0, 0), (0, 0), (0, 0))
    ks = jnp.pad(ks, pad_kv)
    vs = jnp.pad(vs, pad_kv)
    blk_start = jnp.arange(nblk) * B_Q_BLOCK
    key_idx = blk_start[:, None] + jnp.arange(kv_len)[None, :]
    kb = ks[:, key_idx]
    vb = vs[:, key_idx]
    qb = qs.reshape(b, nblk, B_Q_BLOCK, dilation, h, d)
    rel = jnp.arange(kv_len)[None, :] - half_span - jnp.arange(B_Q_BLOCK)[:, None]
    bias = rel_bias_g[t5_bucket(rel * dilation)].astype(jnp.float32).transpose(2, 0, 1)
    key_pos = key_idx - half_span
    valid = (jnp.abs(rel) <= half_span)[None] & ((key_pos >= 0) & (key_pos < L))[:, None, :]
    scale = d ** -0.5
    logits = jnp.einsum('bnqchd,bnkchd->bnchqk', qb, kb).astype(jnp.float32) * scale + bias[None, None, None]
    logits = jnp.where(valid[None, :, None, None], logits, NEG_INF)
    m = jnp.max(logits, axis=-1, keepdims=True)
    p = jnp.exp(logits - m)
    l = jnp.sum(p, axis=-1, keepdims=True)
    o = jnp.einsum('bnchqk,bnkchd->bnqchd', p.astype(v.dtype), vb).astype(jnp.float32)
    l_t = l[..., 0].transpose(0, 1, 4, 2, 3)
    o = o / l_t[..., None]
    log_z = (m[..., 0] + jnp.log(l[..., 0])).transpose(0, 1, 4, 2, 3)
    o = o.reshape(b, Lp, dilation, h, d)[:, :L].reshape(b, s, h, d)
    log_z = log_z.reshape(b, Lp, dilation, h)[:, :L].reshape(b, s, h)
    return o, log_z


def mixer_b(h, w_qkv, w_o, rel_bias):
    b, s, _ = h.shape
    n_g = len(B_GROUPS)
    hg = B_HEADS_PER_GROUP
    qkv = (h @ w_qkv).reshape(b, s, n_g, 3, hg, HEAD_DIM)
    outs, log_zs = [], []
    for g, (window, dil) in enumerate(B_GROUPS):
        o, lz = dilated_group(qkv[:, :, g, 0], qkv[:, :, g, 1], qkv[:, :, g, 2],
                              rel_bias[:, g * hg:(g + 1) * hg], window, dil)
        outs.append(o)
        log_zs.append(lz)
    wts = jax.nn.softmax(jnp.stack(log_zs, axis=0), axis=0)
    y = jnp.concatenate([wts[g][..., None] * outs[g] for g in range(n_g)], axis=2)
    y = y.reshape(b, s, B_WIDTH).astype(h.dtype)
    return y @ w_o


def conv_ffn(h, w_up, conv_w, conv_b, w_down):
    u = h @ w_up
    c = u.shape[-1]
    pad = CONV_WIDTH // 2
    u = lax.conv_general_dilated(u, conv_w[:, None, :].astype(u.dtype), window_strides=(1,),
                                 padding=((pad, pad),), dimension_numbers=('NWC', 'WIO', 'NWC'),
                                 feature_group_count=c) + conv_b
    gate, val = jnp.split(u, 2, axis=-1)
    return (jax.nn.silu(gate) * val) @ w_down


def setup_inputs(seed: int = 0) -> dict:
    key = jax.random.key(seed)
    ks = jax.random.split(key, 16)
    f32 = jnp.float32
    nrm = lambda k, shape, sc: jax.random.normal(k, shape, f32) * sc
    return {
        "x": jax.random.normal(ks[0], (BATCH, SEQ, D_MODEL), f32),
        "a_w_qkv": nrm(ks[1], (N_A_LAYERS, D_MODEL, A_QKV_WIDTH), D_MODEL ** -0.5),
        "a_w_o": nrm(ks[2], (N_A_LAYERS, A_HEADS * HEAD_DIM, D_MODEL), (A_HEADS * HEAD_DIM) ** -0.5),
        "a_q_gain": 1.0 + nrm(ks[3], (N_A_LAYERS, HEAD_DIM), 0.02),
        "a_k_gain": 1.0 + nrm(ks[4], (N_A_LAYERS, HEAD_DIM), 0.02),
        "b_w_qkv": nrm(ks[5], (N_B_LAYERS, D_MODEL, B_QKV_WIDTH), D_MODEL ** -0.5),
        "b_w_o": nrm(ks[6], (N_B_LAYERS, B_WIDTH, D_MODEL), B_WIDTH ** -0.5),
        "rel_bias": nrm(ks[7], (REL_BUCKETS, len(B_GROUPS) * B_HEADS_PER_GROUP), 0.5),
        "mix_norm": 1.0 + nrm(ks[8], (DEPTH, D_MODEL), 0.02),
        "ffn_norm": 1.0 + nrm(ks[9], (DEPTH, D_MODEL), 0.02),
        "w_up": nrm(ks[10], (DEPTH, D_MODEL, 2 * D_FF), D_MODEL ** -0.5),
        "conv_w": nrm(ks[11], (DEPTH, CONV_WIDTH, 2 * D_FF), CONV_WIDTH ** -0.5),
        "conv_b": nrm(ks[12], (DEPTH, 2 * D_FF), 0.01),
        "w_down": nrm(ks[13], (DEPTH, D_FF, D_MODEL), D_FF ** -0.5),
        "final_norm": 1.0 + nrm(ks[14], (D_MODEL,), 0.02),
    }


def reference(x, a_w_qkv, a_w_o, a_q_gain, a_k_gain, b_w_qkv, b_w_o, rel_bias,
              mix_norm, ffn_norm, w_up, conv_w, conv_b, w_down, final_norm):
    h = x
    for i in range(DEPTH):
        hn = rms_norm(h, mix_norm[i])
        j = i // N_MIXERS
        if i % N_MIXERS == 0:
            h = h + mixer_a(hn, a_w_qkv[j], a_w_o[j], a_q_gain[j], a_k_gain[j])
        else:
            h = h + mixer_b(hn, b_w_qkv[j], b_w_o[j], rel_bias)
        h = h + conv_ffn(rms_norm(h, ffn_norm[i]), w_up[i], conv_w[i], conv_b[i], w_down[i])
    return rms_norm(h, final_norm)
```

```python
import functools
import math

import jax
import jax.numpy as jnp
from jax import lax
from jax.experimental import pallas as pl
from jax.experimental.pallas import tpu as pltpu

GRID_W = 64
HEAD_DIM = 128
A_HEADS = 16
A_KV_HEADS = 4
A_GROUP = A_HEADS // A_KV_HEADS
ROPE_THETA = 10000.0
B_GROUPS = ((128, 1), (512, 4), (2048, 16))
B_HEADS_PER_GROUP = 8
B_GROUP_WIDTH = B_HEADS_PER_GROUP * HEAD_DIM
REL_BUCKETS = 32
REL_MAX_DISTANCE = 1024
EPS = 1e-6
NEG_INF = -1e30
LOG2E = 1.4426950408889634

V7X_VMEM_LIMIT_BYTES = 56 * 1024 * 1024
BF16_SUBLANES = 16
LANES = 128

TM = 1024
TN = 512
A_TQ = 256
A_TK = 512
B_TQ = 128
B_HALO = 64
CONV_HALO = BF16_SUBLANES

F32 = jnp.float32
BF16 = jnp.bfloat16


def _cparams(*sem):
    return pltpu.CompilerParams(dimension_semantics=sem, vmem_limit_bytes=V7X_VMEM_LIMIT_BYTES)


def _rms(x, gain):
    ms = jnp.mean(x * x, axis=-1, keepdims=True)
    return x * lax.rsqrt(ms + EPS) * gain


def _qkv_a_kernel(x_ref, g_ref, w_ref, cos_ref, sin_ref, qg_ref, kg_ref, o_ref, xn_ref, *, n_qk_tiles, n_q_tiles):
    j = pl.program_id(1)

    @pl.when(j == 0)
    def _():
        xn_ref[...] = _rms(x_ref[...], g_ref[...]).astype(BF16)

    y = jnp.dot(xn_ref[...], w_ref[...], preferred_element_type=F32)

    @pl.when(j < n_qk_tiles)
    def _():
        gain = jnp.where(j < n_q_tiles, qg_ref[...], kg_ref[...])
        cos = cos_ref[...]
        sin = sin_ref[...]
        lane = lax.broadcasted_iota(jnp.int32, (1, HEAD_DIM), 1)
        first_quarter = (lane % (HEAD_DIM // 2)) < (HEAD_DIM // 4)
        for hh in range(y.shape[1] // HEAD_DIM):
            yh = y[:, hh * HEAD_DIM:(hh + 1) * HEAD_DIM]
            yn = _rms(yh, gain)
            partner = jnp.where(first_quarter,
                                pltpu.roll(yn, HEAD_DIM - HEAD_DIM // 4, 1),
                                pltpu.roll(yn, HEAD_DIM // 4, 1))
            o_ref[:, hh * HEAD_DIM:(hh + 1) * HEAD_DIM] = (yn * cos + partner * sin).astype(BF16)

    @pl.when(j >= n_qk_tiles)
    def _():
        o_ref[...] = y.astype(BF16)


def _qkv_a(h, gain, w, cos_t, sin_t, q_gain, k_gain, seq):
    m, d = h.shape
    n = w.shape[1]
    tm, tn = min(TM, seq), TN
    nq = A_HEADS * HEAD_DIM
    nk = A_KV_HEADS * HEAD_DIM
    assert m % tm == 0 and seq % tm == 0 and n % tn == 0 and nq % tn == 0 and (nq + nk) % tn == 0
    pos_blocks = seq // tm
    kern = functools.partial(_qkv_a_kernel, n_qk_tiles=(nq + nk) // tn, n_q_tiles=nq // tn)
    return pl.pallas_call(
        kern,
        grid=(m // tm, n // tn),
        in_specs=[
            pl.BlockSpec((tm, d), lambda i, j: (i, 0)),
            pl.BlockSpec((1, d), lambda i, j: (0, 0)),
            pl.BlockSpec((d, tn), lambda i, j: (0, j)),
            pl.BlockSpec((tm, HEAD_DIM), lambda i, j: (i % pos_blocks, 0)),
            pl.BlockSpec((tm, HEAD_DIM), lambda i, j: (i % pos_blocks, 0)),
            pl.BlockSpec((1, HEAD_DIM), lambda i, j: (0, 0)),
            pl.BlockSpec((1, HEAD_DIM), lambda i, j: (0, 0)),
        ],
        out_specs=pl.BlockSpec((tm, tn), lambda i, j: (i, j)),
        out_shape=jax.ShapeDtypeStruct((m, n), BF16),
        scratch_shapes=[pltpu.VMEM((tm, d), BF16)],
        compiler_params=_cparams("parallel", "arbitrary"),
        name="qkv_a",
    )(h, gain, w, cos_t, sin_t, q_gain, k_gain)


def _attn_a_kernel(q_ref, k_ref, v_ref, o_ref, *, tk):
    tq = q_ref.shape[0]
    seq = k_ref.shape[0]
    q = jnp.concatenate([q_ref[:, g * HEAD_DIM:(g + 1) * HEAD_DIM] for g in range(A_GROUP)], axis=0)
    rows = A_GROUP * tq
    c = (HEAD_DIM ** -0.5) * LOG2E

    def body(t, carry):
        m, l, acc = carry
        start = pl.multiple_of(t * tk, tk)
        k = k_ref[pl.ds(start, tk), :]
        v = v_ref[pl.ds(start, tk), :]
        s = lax.dot_general(q, k, (((1,), (1,)), ((), ())), preferred_element_type=F32)
        m_new = jnp.maximum(m, jnp.max(s, axis=-1, keepdims=True))
        alpha = jnp.exp2((m - m_new) * c)
        p = jnp.exp2((s - m_new) * c)
        l = alpha * l + jnp.sum(p, axis=-1, keepdims=True)
        acc = alpha * acc + jnp.dot(p.astype(BF16), v, preferred_element_type=F32)
        return m_new, l, acc

    m0 = jnp.full((rows, 1), NEG_INF, F32)
    l0 = jnp.zeros((rows, 1), F32)
    a0 = jnp.zeros((rows, HEAD_DIM), F32)
    _, l, acc = lax.fori_loop(0, seq // tk, body, (m0, l0, a0))
    out = acc / l
    for g in range(A_GROUP):
        o_ref[:, g * HEAD_DIM:(g + 1) * HEAD_DIM] = out[g * tq:(g + 1) * tq].astype(BF16)


def _attn_a(qkv, batch, seq):
    m = qkv.shape[0]
    tq = min(A_TQ, seq)
    tk = min(A_TK, seq)
    assert seq % tq == 0 and seq % tk == 0
    gw = A_GROUP * HEAD_DIM
    q_blocks = seq // tq
    k_col0 = A_HEADS
    v_col0 = A_HEADS + A_KV_HEADS
    return pl.pallas_call(
        functools.partial(_attn_a_kernel, tk=tk),
        grid=(batch, A_KV_HEADS, q_blocks),
        in_specs=[
            pl.BlockSpec((tq, gw), lambda b, kv, qi: (b * q_blocks + qi, kv)),
            pl.BlockSpec((seq, HEAD_DIM), lambda b, kv, qi: (b, k_col0 + kv)),
            pl.BlockSpec((seq, HEAD_DIM), lambda b, kv, qi: (b, v_col0 + kv)),
        ],
        out_specs=pl.BlockSpec((tq, gw), lambda b, kv, qi: (b * q_blocks + qi, kv)),
        out_shape=jax.ShapeDtypeStruct((m, A_HEADS * HEAD_DIM), BF16),
        compiler_params=_cparams("parallel", "parallel", "arbitrary"),
        name="attn_a",
    )(qkv, qkv, qkv)


def _matmul_resid_kernel(x_ref, w_ref, r_ref, o_ref):
    o_ref[...] = r_ref[...] + jnp.dot(x_ref[...], w_ref[...], preferred_element_type=F32)


def _matmul_resid(x, w, resid, tm, tn, name):
    m, k = x.shape
    n = w.shape[1]
    tm, tn = min(tm, m), min(tn, n)
    assert m % tm == 0 and n % tn == 0
    return pl.pallas_call(
        _matmul_resid_kernel,
        grid=(m // tm, n // tn),
        in_specs=[
            pl.BlockSpec((tm, k), lambda i, j: (i, 0)),
            pl.BlockSpec((k, tn), lambda i, j: (0, j)),
            pl.BlockSpec((tm, tn), lambda i, j: (i, j)),
        ],
        out_specs=pl.BlockSpec((tm, tn), lambda i, j: (i, j)),
        out_shape=jax.ShapeDtypeStruct((m, n), F32),
        compiler_params=_cparams("parallel", "parallel"),
        name=name,
    )(x, w, resid)


def _norm_matmul_kernel(x_ref, g_ref, w_ref, o_ref, xn_ref):
    @pl.when(pl.program_id(1) == 0)
    def _():
        xn_ref[...] = _rms(x_ref[...], g_ref[...]).astype(BF16)

    o_ref[...] = jnp.dot(xn_ref[...], w_ref[...], preferred_element_type=F32).astype(o_ref.dtype)


def _norm_matmul(h, gain, w, tm, tn, name):
    m, d = h.shape
    n = w.shape[1]
    tm, tn = min(tm, m), min(tn, n)
    assert m % tm == 0 and n % tn == 0
    return pl.pallas_call(
        _norm_matmul_kernel,
        grid=(m // tm, n // tn),
        in_specs=[
            pl.BlockSpec((tm, d), lambda i, j: (i, 0)),
            pl.BlockSpec((1, d), lambda i, j: (0, 0)),
            pl.BlockSpec((d, tn), lambda i, j: (0, j)),
        ],
        out_specs=pl.BlockSpec((tm, tn), lambda i, j: (i, j)),
        out_shape=jax.ShapeDtypeStruct((m, n), BF16),
        scratch_shapes=[pltpu.VMEM((tm, d), BF16)],
        compiler_params=_cparams("parallel", "arbitrary"),
        name=name,
    )(h, gain, w)


def _t5_bucket(rel):
    nb = REL_BUCKETS // 2
    max_exact = nb // 2
    base = jnp.where(rel > 0, nb, 0)
    n = jnp.abs(rel)
    nf = jnp.maximum(n, 1).astype(F32)
    large = max_exact + (jnp.log(nf / max_exact) / math.log(REL_MAX_DISTANCE / max_exact)
                         * (nb - max_exact)).astype(jnp.int32)
    large = jnp.minimum(large, nb - 1)
    return base + jnp.where(n < max_exact, n, large)


def _b_bias_tables(rel_bias, g, dil, tq):
    w = tq + 2 * B_HALO
    kk = jnp.arange(w)
    rel = kk[None, :] - B_HALO - jnp.arange(tq)[:, None]
    hg = B_HEADS_PER_GROUP
    bias = rel_bias[:, g * hg:(g + 1) * hg][_t5_bucket(rel * dil)].astype(F32).transpose(2, 0, 1)
    band = jnp.abs(rel) <= B_HALO
    variants = []
    for first, last in ((False, False), (True, False), (False, True), (True, True)):
        ok = band
        if first:
            ok = ok & (kk >= B_HALO)[None, :]
        if last:
            ok = ok & (kk < tq + B_HALO)[None, :]
        variants.append(jnp.where(ok[None], bias, NEG_INF))
    return jnp.stack(variants)


def _attn_b_kernel(q_ref, kp_ref, kc_ref, kn_ref, vp_ref, vc_ref, vn_ref, bias_ref, o_ref, lz_ref):
    tq = q_ref.shape[1]
    scale = HEAD_DIM ** -0.5
    lane_head = lax.broadcasted_iota(jnp.int32, (1, LANES), 1) // (LANES // B_HEADS_PER_GROUP)
    lz_all = jnp.zeros((tq, LANES), F32)
    for hh in range(B_HEADS_PER_GROUP):
        cs = slice(hh * HEAD_DIM, (hh + 1) * HEAD_DIM)
        q = q_ref[0, :, cs]
        kwin = jnp.concatenate([kp_ref[0, :, cs], kc_ref[0, :, cs], kn_ref[0, :, cs]], axis=0)
        vwin = jnp.concatenate([vp_ref[0, :, cs], vc_ref[0, :, cs], vn_ref[0, :, cs]], axis=0)
        s = lax.dot_general(q, kwin, (((1,), (1,)), ((), ())), preferred_element_type=F32)
        s = s * scale + bias_ref[0, hh]
        m = jnp.max(s, axis=-1, keepdims=True)
        p = jnp.exp(s - m)
        l = jnp.sum(p, axis=-1, keepdims=True)
        o = jnp.dot(p.astype(BF16), vwin, preferred_element_type=F32)
        o_ref[0, :, cs] = o / l
        lz_all = jnp.where(lane_head == hh, m + jnp.log(l), lz_all)
    lz_ref[0] = lz_all


def _attn_b_group(qkv, rel_bias, g, dil, batch, seq):
    m, width = qkv.shape
    ln = seq // dil
    tq = min(B_TQ, ln)
    assert ln % tq == 0 and tq % B_HALO == 0 and ln >= tq
    n_tiles = ln // tq
    hpt = tq // B_HALO
    halo_blocks = ln // B_HALO
    w = tq + 2 * B_HALO
    cb = width // B_GROUP_WIDTH
    view = qkv.reshape(batch, ln, dil * width)
    bias = _b_bias_tables(rel_bias, g, dil, tq)

    def col(which):
        return lambda c: c * cb + g * 3 + which

    def cur(which):
        f = col(which)
        return pl.BlockSpec((1, tq, B_GROUP_WIDTH), lambda b, c, i: (b, i, f(c)))

    def prev(which):
        f = col(which)
        return pl.BlockSpec((1, B_HALO, B_GROUP_WIDTH), lambda b, c, i: (b, jnp.maximum(i * hpt - 1, 0), f(c)))

    def nxt(which):
        f = col(which)
        return pl.BlockSpec((1, B_HALO, B_GROUP_WIDTH),
                            lambda b, c, i: (b, jnp.minimum((i + 1) * hpt, halo_blocks - 1), f(c)))

    def variant(i):
        return (i == 0).astype(jnp.int32) + 2 * (i == n_tiles - 1).astype(jnp.int32)

    o, lz = pl.pallas_call(
        _attn_b_kernel,
        grid=(batch, dil, n_tiles),
        in_specs=[
            cur(0), prev(1), cur(1), nxt(1), prev(2), cur(2), nxt(2),
            pl.BlockSpec((1, B_HEADS_PER_GROUP, tq, w), lambda b, c, i: (variant(i), 0, 0, 0)),
        ],
        out_specs=[
            pl.BlockSpec((1, tq, B_GROUP_WIDTH), lambda b, c, i: (b, i, c)),
            pl.BlockSpec((1, tq, LANES), lambda b, c, i: (b, i, c)),
        ],
        out_shape=[
            jax.ShapeDtypeStruct((batch, ln, dil * B_GROUP_WIDTH), F32),
            jax.ShapeDtypeStruct((batch, ln, dil * LANES), F32),
        ],
        compiler_params=_cparams("parallel", "parallel", "parallel"),
        name=f"attn_b_g{g}",
    )(view, view, view, view, view, view, view, bias)
    return o.reshape(m, B_GROUP_WIDTH), lz.reshape(m, LANES)


def _mix_proj_kernel(o0_ref, o1_ref, o2_ref, z0_ref, z1_ref, z2_ref, w_ref, r_ref, out_ref, y_ref):
    @pl.when(pl.program_id(1) == 0)
    def _():
        z = [z0_ref[...], z1_ref[...], z2_ref[...]]
        zmax = jnp.maximum(jnp.maximum(z[0], z[1]), z[2])
        e = [jnp.exp(zz - zmax) for zz in z]
        den = e[0] + e[1] + e[2]
        lanes_per_head = LANES // B_HEADS_PER_GROUP
        for g, o_ref in enumerate((o0_ref, o1_ref, o2_ref)):
            wt = e[g] / den
            for hh in range(B_HEADS_PER_GROUP):
                wcol = wt[:, hh * lanes_per_head:hh * lanes_per_head + 1]
                cs = slice(hh * HEAD_DIM, (hh + 1) * HEAD_DIM)
                y_ref[:, g * B_GROUP_WIDTH + hh * HEAD_DIM:g * B_GROUP_WIDTH + (hh + 1) * HEAD_DIM] = (
                    wcol * o_ref[:, cs]).astype(BF16)

    out_ref[...] = r_ref[...] + jnp.dot(y_ref[...], w_ref[...], preferred_element_type=F32)


def _mix_proj(outs, lzs, w, resid, tm, tn):
    m = resid.shape[0]
    k, n = w.shape
    tm, tn = min(tm, m), min(tn, n)
    assert m % tm == 0 and n % tn == 0
    o_spec = pl.BlockSpec((tm, B_GROUP_WIDTH), lambda i, j: (i, 0))
    z_spec = pl.BlockSpec((tm, LANES), lambda i, j: (i, 0))
    return pl.pallas_call(
        _mix_proj_kernel,
        grid=(m // tm, n // tn),
        in_specs=[o_spec, o_spec, o_spec, z_spec, z_spec, z_spec,
                  pl.BlockSpec((k, tn), lambda i, j: (0, j)),
                  pl.BlockSpec((tm, tn), lambda i, j: (i, j))],
        out_specs=pl.BlockSpec((tm, tn), lambda i, j: (i, j)),
        out_shape=jax.ShapeDtypeStruct((m, n), F32),
        scratch_shapes=[pltpu.VMEM((tm, k), BF16)],
        compiler_params=_cparams("parallel", "arbitrary"),
        name="mix_proj_b",
    )(*outs, *lzs, w, resid)


def _ffn_up_kernel(xp_ref, x_ref, xn_ref, g_ref, wg_ref, wv_ref, cwg_ref, cwv_ref, cbg_ref, cbv_ref,
                   o_ref, xs_ref, *, seq):
    i = pl.program_id(0)
    tm = x_ref.shape[0]
    hl = CONV_HALO
    ext = tm + 2 * hl

    @pl.when(pl.program_id(1) == 0)
    def _():
        gain = g_ref[...]
        at_start = (i * tm) % seq == 0
        at_end = ((i + 1) * tm) % seq == 0
        xs_ref[0:hl, :] = jnp.where(at_start, 0.0, _rms(xp_ref[...], gain)).astype(BF16)
        xs_ref[hl:hl + tm, :] = _rms(x_ref[...], gain).astype(BF16)
        xs_ref[hl + tm:ext, :] = jnp.where(at_end, 0.0, _rms(xn_ref[...], gain)).astype(BF16)

    xs = xs_ref[...]

    def conv(w_ref, cw_ref, cb_ref):
        u = jnp.dot(xs, w_ref[...], preferred_element_type=F32)
        below = pltpu.roll(u, 1, 0)[hl:hl + tm]
        above = pltpu.roll(u, ext - 1, 0)[hl:hl + tm]
        cw = cw_ref[...]
        return below * cw[0:1] + u[hl:hl + tm] * cw[1:2] + above * cw[2:3] + cb_ref[...]

    gate = conv(wg_ref, cwg_ref, cbg_ref)
    val = conv(wv_ref, cwv_ref, cbv_ref)
    o_ref[...] = (gate * jax.nn.sigmoid(gate) * val).astype(BF16)


def _ffn_up(h, gain, w_up, conv_w, conv_b, seq, tm, tn):
    m, d = h.shape
    dff = w_up.shape[1] // 2
    tm, tn = min(tm, seq), min(tn, dff)
    hl = CONV_HALO
    assert m % tm == 0 and seq % tm == 0 and dff % tn == 0 and tm % hl == 0
    nj = dff // tn
    rpt = tm // hl
    last_halo = m // hl - 1
    return pl.pallas_call(
        functools.partial(_ffn_up_kernel, seq=seq),
        grid=(m // tm, nj),
        in_specs=[
            pl.BlockSpec((hl, d), lambda i, j: (jnp.maximum(i * rpt - 1, 0), 0)),
            pl.BlockSpec((tm, d), lambda i, j: (i, 0)),
            pl.BlockSpec((hl, d), lambda i, j: (jnp.minimum((i + 1) * rpt, last_halo), 0)),
            pl.BlockSpec((1, d), lambda i, j: (0, 0)),
            pl.BlockSpec((d, tn), lambda i, j: (0, j)),
            pl.BlockSpec((d, tn), lambda i, j: (0, nj + j)),
            pl.BlockSpec((3, tn), lambda i, j: (0, j)),
            pl.BlockSpec((3, tn), lambda i, j: (0, nj + j)),
            pl.BlockSpec((1, tn), lambda i, j: (0, j)),
            pl.BlockSpec((1, tn), lambda i, j: (0, nj + j)),
        ],
        out_specs=pl.BlockSpec((tm, tn), lambda i, j: (i, j)),
        out_shape=jax.ShapeDtypeStruct((m, dff), BF16),
        scratch_shapes=[pltpu.VMEM((tm + 2 * hl, d), BF16)],
        compiler_params=_cparams("parallel", "arbitrary"),
        name="ffn_up",
    )(h, h, h, gain, w_up, w_up, conv_w, conv_w, conv_b, conv_b)


def _final_norm_kernel(x_ref, g_ref, o_ref):
    o_ref[...] = _rms(x_ref[...], g_ref[...])


def _final_norm(h, gain, tm):
    m, d = h.shape
    tm = min(tm, m)
    assert m % tm == 0
    return pl.pallas_call(
        _final_norm_kernel,
        grid=(m // tm,),
        in_specs=[pl.BlockSpec((tm, d), lambda i: (i, 0)), pl.BlockSpec((1, d), lambda i: (0, 0))],
        out_specs=pl.BlockSpec((tm, d), lambda i: (i, 0)),
        out_shape=jax.ShapeDtypeStruct((m, d), F32),
        compiler_params=_cparams("parallel"),
        name="final_norm",
    )(h, gain)


def _rope_tables(seq):
    rows = seq // GRID_W
    row_ids = jnp.repeat(jnp.arange(rows, dtype=F32), GRID_W)
    col_ids = jnp.tile(jnp.arange(GRID_W, dtype=F32), rows)
    quarter = HEAD_DIM // 4
    inv_freq = ROPE_THETA ** (-jnp.arange(quarter, dtype=F32) / quarter)
    ang_r = row_ids[:, None] * inv_freq[None, :]
    ang_c = col_ids[:, None] * inv_freq[None, :]
    cos_t = jnp.concatenate([jnp.cos(ang_r), jnp.cos(ang_r), jnp.cos(ang_c), jnp.cos(ang_c)], axis=-1)
    sin_t = jnp.concatenate([-jnp.sin(ang_r), jnp.sin(ang_r), -jnp.sin(ang_c), jnp.sin(ang_c)], axis=-1)
    return cos_t, sin_t


def kernel(x, a_w_qkv, a_w_o, a_q_gain, a_k_gain, b_w_qkv, b_w_o, rel_bias, mix_norm, ffn_norm, w_up, conv_w,
           conv_b, w_down, final_norm):
    batch, seq, d = x.shape
    m = batch * seq
    depth = mix_norm.shape[0]
    h = x.reshape(m, d)
    cos_t, sin_t = _rope_tables(seq)
    for i in range(depth):
        jj = i // 2
        gain = mix_norm[i][None, :]
        if i % 2 == 0:
            qkv = _qkv_a(h, gain, a_w_qkv[jj].astype(BF16), cos_t, sin_t,
                         a_q_gain[jj][None, :], a_k_gain[jj][None, :], seq)
            o = _attn_a(qkv, batch, seq)
            h = _matmul_resid(o, a_w_o[jj].astype(BF16), h, TM, TN, "out_proj_a")
        else:
            qkv = _norm_matmul(h, gain, b_w_qkv[jj].astype(BF16), TM, 2 * TN, "qkv_b")
            outs, lzs = [], []
            for g, (_, dil) in enumerate(B_GROUPS):
                o, lz = _attn_b_group(qkv, rel_bias, g, dil, batch, seq)
                outs.append(o)
                lzs.append(lz)
            h = _mix_proj(outs, lzs, b_w_o[jj].astype(BF16), h, TM, TN)
        act = _ffn_up(h, ffn_norm[i][None, :], w_up[i].astype(BF16), conv_w[i], conv_b[i][None, :], seq, TM, TN)
        h = _matmul_resid(act, w_down[i].astype(BF16), h, TM // 2, TN, "ffn_down")
    return _final_norm(h, final_norm[None, :], TM // 2).reshape(batch, seq, d)
```

```python
import functools
import math

import jax
import jax.numpy as jnp
from jax import lax
from jax.experimental import pallas as pl
from jax.experimental.pallas import tpu as pltpu

GRID_W = 64
HEAD_DIM = 128
A_HEADS = 16
A_KV_HEADS = 4
A_GROUP = A_HEADS // A_KV_HEADS
ROPE_THETA = 10000.0
B_GROUPS = ((128, 1), (512, 4), (2048, 16))
B_HEADS_PER_GROUP = 8
B_GROUP_WIDTH = B_HEADS_PER_GROUP * HEAD_DIM
REL_BUCKETS = 32
REL_MAX_DISTANCE = 1024
EPS = 1e-6
NEG_INF = -1e30
LOG2E = 1.4426950408889634

V7X_VMEM_LIMIT_BYTES = 56 * 1024 * 1024
BF16_SUBLANES = 16
LANES = 128

TM = 1024
TN = 512
A_TQ = 256
A_TK = 512
B_TQ = 128
B_HALO = 64
CONV_HALO = BF16_SUBLANES

F32 = jnp.float32
BF16 = jnp.bfloat16


def _cparams(*sem):
    return pltpu.CompilerParams(dimension_semantics=sem, vmem_limit_bytes=V7X_VMEM_LIMIT_BYTES)


def _rms(x, gain):
    ms = jnp.mean(x * x, axis=-1, keepdims=True)
    return x * lax.rsqrt(ms + EPS) * gain


def _qkv_a_kernel(x_ref, g_ref, w_ref, cos_ref, sin_ref, qg_ref, kg_ref, o_ref, xn_ref, *, n_qk_tiles, n_q_tiles):
    j = pl.program_id(1)

    @pl.when(j == 0)
    def _():
        xn_ref[...] = _rms(x_ref[...], g_ref[...]).astype(BF16)

    y = jnp.dot(xn_ref[...], w_ref[...], preferred_element_type=F32)

    @pl.when(j < n_qk_tiles)
    def _():
        gain = jnp.where(j < n_q_tiles, qg_ref[...], kg_ref[...])
        cos = cos_ref[...]
        sin = sin_ref[...]
        lane = lax.broadcasted_iota(jnp.int32, (1, HEAD_DIM), 1)
        first_quarter = (lane % (HEAD_DIM // 2)) < (HEAD_DIM // 4)
        for hh in range(y.shape[1] // HEAD_DIM):
            yh = y[:, hh * HEAD_DIM:(hh + 1) * HEAD_DIM]
            yn = _rms(yh, gain)
            partner = jnp.where(first_quarter,
                                pltpu.roll(yn, HEAD_DIM - HEAD_DIM // 4, 1),
                                pltpu.roll(yn, HEAD_DIM // 4, 1))
            o_ref[:, hh * HEAD_DIM:(hh + 1) * HEAD_DIM] = (yn * cos + partner * sin).astype(BF16)

    @pl.when(j >= n_qk_tiles)
    def _():
        o_ref[...] = y.astype(BF16)


def _qkv_a(h, gain, w, cos_t, sin_t, q_gain, k_gain, seq):
    m, d = h.shape
    n = w.shape[1]
    tm, tn = min(TM, seq), TN
    nq = A_HEADS * HEAD_DIM
    nk = A_KV_HEADS * HEAD_DIM
    assert m % tm == 0 and seq % tm == 0 and n % tn == 0 and nq % tn == 0 and (nq + nk) % tn == 0
    pos_blocks = seq // tm
    kern = functools.partial(_qkv_a_kernel, n_qk_tiles=(nq + nk) // tn, n_q_tiles=nq // tn)
    return pl.pallas_call(
        kern,
        grid=(m // tm, n // tn),
        in_specs=[
            pl.BlockSpec((tm, d), lambda i, j: (i, 0)),
            pl.BlockSpec((1, d), lambda i, j: (0, 0)),
            pl.BlockSpec((d, tn), lambda i, j: (0, j)),
            pl.BlockSpec((tm, HEAD_DIM), lambda i, j: (i % pos_blocks, 0)),
            pl.BlockSpec((tm, HEAD_DIM), lambda i, j: (i % pos_blocks, 0)),
            pl.BlockSpec((1, HEAD_DIM), lambda i, j: (0, 0)),
            pl.BlockSpec((1, HEAD_DIM), lambda i, j: (0, 0)),
        ],
        out_specs=pl.BlockSpec((tm, tn), lambda i, j: (i, j)),
        out_shape=jax.ShapeDtypeStruct((m, n), BF16),
        scratch_shapes=[pltpu.VMEM((tm, d), BF16)],
        compiler_params=_cparams("parallel", "arbitrary"),
        name="qkv_a",
    )(h, gain, w, cos_t, sin_t, q_gain, k_gain)


def _attn_a_kernel(q_ref, k_ref, v_ref, o_ref, *, tk):
    tq = q_ref.shape[0]
    seq = k_ref.shape[0]
    q = jnp.concatenate([q_ref[:, g * HEAD_DIM:(g + 1) * HEAD_DIM] for g in range(A_GROUP)], axis=0)
    rows = A_GROUP * tq
    c = (HEAD_DIM ** -0.5) * LOG2E

    def body(t, carry):
        m, l, acc = carry
        start = pl.multiple_of(t * tk, tk)
        k = k_ref[pl.ds(start, tk), :]
        v = v_ref[pl.ds(start, tk), :]
        s = lax.dot_general(q, k, (((1,), (1,)), ((), ())), preferred_element_type=F32)
        m_new = jnp.maximum(m, jnp.max(s, axis=-1, keepdims=True))
        alpha = jnp.exp2((m - m_new) * c)
        p = jnp.exp2((s - m_new) * c)
        l = alpha * l + jnp.sum(p, axis=-1, keepdims=True)
        acc = alpha * acc + jnp.dot(p.astype(BF16), v, preferred_element_type=F32)
        return m_new, l, acc

    m0 = jnp.full((rows, 1), NEG_INF, F32)
    l0 = jnp.zeros((rows, 1), F32)
    a0 = jnp.zeros((rows, HEAD_DIM), F32)
    _, l, acc = lax.fori_loop(0, seq // tk, body, (m0, l0, a0))
    out = acc / l
    for g in range(A_GROUP):
        o_ref[:, g * HEAD_DIM:(g + 1) * HEAD_DIM] = out[g * tq:(g + 1) * tq].astype(BF16)


def _attn_a(qkv, batch, seq):
    m = qkv.shape[0]
    tq = min(A_TQ, seq)
    tk = min(A_TK, seq)
    assert seq % tq == 0 and seq % tk == 0
    gw = A_GROUP * HEAD_DIM
    q_blocks = seq // tq
    k_col0 = A_HEADS
    v_col0 = A_HEADS + A_KV_HEADS
    return pl.pallas_call(
        functools.partial(_attn_a_kernel, tk=tk),
        grid=(batch, A_KV_HEADS, q_blocks),
        in_specs=[
            pl.BlockSpec((tq, gw), lambda b, kv, qi: (b * q_blocks + qi, kv)),
            pl.BlockSpec((seq, HEAD_DIM), lambda b, kv, qi: (b, k_col0 + kv)),
            pl.BlockSpec((seq, HEAD_DIM), lambda b, kv, qi: (b, v_col0 + kv)),
        ],
        out_specs=pl.BlockSpec((tq, gw), lambda b, kv, qi: (b * q_blocks + qi, kv)),
        out_shape=jax.ShapeDtypeStruct((m, A_HEADS * HEAD_DIM), BF16),
        compiler_params=_cparams("parallel", "parallel", "arbitrary"),
        name="attn_a",
    )(qkv, qkv, qkv)


def _matmul_resid_kernel(x_ref, w_ref, r_ref, o_ref):
    o_ref[...] = r_ref[...] + jnp.dot(x_ref[...], w_ref[...], preferred_element_type=F32)


def _matmul_resid(x, w, resid, tm, tn, name):
    m, k = x.shape
    n = w.shape[1]
    tm, tn = min(tm, m), min(tn, n)
    assert m % tm == 0 and n % tn == 0
    return pl.pallas_call(
        _matmul_resid_kernel,
        grid=(m // tm, n // tn),
        in_specs=[
            pl.BlockSpec((tm, k), lambda i, j: (i, 0)),
            pl.BlockSpec((k, tn), lambda i, j: (0, j)),
            pl.BlockSpec((tm, tn), lambda i, j: (i, j)),
        ],
        out_specs=pl.BlockSpec((tm, tn), lambda i, j: (i, j)),
        out_shape=jax.ShapeDtypeStruct((m, n), F32),
        compiler_params=_cparams("parallel", "parallel"),
        name=name,
    )(x, w, resid)


def _qkv_b_kernel(x_ref, g_ref, w_ref, o_ref, xs_ref, *lane_chunks, dil):
    tm, d = x_ref.shape
    n = tm // dil

    @pl.when(pl.program_id(1) == 0)
    def _():
        gain = g_ref[...]
        if dil == 1:
            xs_ref[...] = _rms(x_ref[...], gain).astype(BF16)
        else:
            xc_ref, = lane_chunks
            for k in range(d // LANES):
                xc_ref[k] = x_ref[:, k * LANES:(k + 1) * LANES]
            for c in range(dil):
                rows = jnp.concatenate(
                    [xc_ref[k, pl.ds(c, n, stride=dil), :] for k in range(d // LANES)], axis=1)
                xs_ref[c * n:(c + 1) * n, :] = _rms(rows, gain).astype(BF16)

    y = jnp.dot(xs_ref[...], w_ref[...], preferred_element_type=F32)
    for c in range(dil):
        o_ref[0, c] = y[c * n:(c + 1) * n].astype(BF16)


def _qkv_b_group(h, gain, w, g, dil, batch, seq, tm, tn):
    m, d = h.shape
    gw = 3 * B_GROUP_WIDTH
    tm = min(tm, seq)
    assert seq % tm == 0 and gw % tn == 0 and tm % (dil * BF16_SUBLANES) == 0
    tiles_per_seq = seq // tm
    n = tm // dil
    col0 = g * (gw // tn)
    return pl.pallas_call(
        functools.partial(_qkv_b_kernel, dil=dil),
        grid=(m // tm, gw // tn),
        in_specs=[
            pl.BlockSpec((tm, d), lambda i, j: (i, 0)),
            pl.BlockSpec((1, d), lambda i, j: (0, 0)),
            pl.BlockSpec((d, tn), lambda i, j: (0, col0 + j)),
        ],
        out_specs=pl.BlockSpec((1, dil, n, tn), lambda i, j: (i // tiles_per_seq, 0, i % tiles_per_seq, j)),
        out_shape=jax.ShapeDtypeStruct((batch, dil, seq // dil, gw), BF16),
        scratch_shapes=[pltpu.VMEM((tm, d), BF16)] + (
            [pltpu.VMEM((d // LANES, tm, LANES), F32)] if dil > 1 else []),
        compiler_params=_cparams("parallel", "arbitrary"),
        name=f"qkv_b_g{g}",
    )(h, gain, w)


def _t5_bucket(rel):
    nb = REL_BUCKETS // 2
    max_exact = nb // 2
    base = jnp.where(rel > 0, nb, 0)
    n = jnp.abs(rel)
    nf = jnp.maximum(n, 1).astype(F32)
    large = max_exact + (jnp.log(nf / max_exact) / math.log(REL_MAX_DISTANCE / max_exact)
                         * (nb - max_exact)).astype(jnp.int32)
    large = jnp.minimum(large, nb - 1)
    return base + jnp.where(n < max_exact, n, large)


def _b_bias_tables(rel_bias, g, dil, tq):
    w = tq + 2 * B_HALO
    kk = jnp.arange(w)
    rel = kk[None, :] - B_HALO - jnp.arange(tq)[:, None]
    hg = B_HEADS_PER_GROUP
    bias = rel_bias[:, g * hg:(g + 1) * hg][_t5_bucket(rel * dil)].astype(F32).transpose(2, 0, 1)
    bias = jnp.where((jnp.abs(rel) <= B_HALO)[None], bias, NEG_INF)
    edges = []
    for first, last in ((False, False), (True, False), (False, True), (True, True)):
        off = jnp.zeros((w,), bool)
        if first:
            off = off | (kk < B_HALO)
        if last:
            off = off | (kk >= tq + B_HALO)
        edges.append(jnp.where(off, NEG_INF, 0.0).astype(F32)[None, :])
    return bias, jnp.stack(edges)


def _attn_b_kernel(q_ref, kp_ref, kc_ref, kn_ref, vp_ref, vc_ref, vn_ref, bias_ref, edge_ref, o_ref, lz_ref):
    tq = q_ref.shape[2]
    scale = HEAD_DIM ** -0.5
    lane_head = lax.broadcasted_iota(jnp.int32, (1, LANES), 1) // (LANES // B_HEADS_PER_GROUP)
    lz_all = jnp.zeros((tq, LANES), F32)
    edge = edge_ref[0]
    for hh in range(B_HEADS_PER_GROUP):
        cs = slice(hh * HEAD_DIM, (hh + 1) * HEAD_DIM)
        q = q_ref[0, 0, :, cs]
        kwin = jnp.concatenate([kp_ref[0, 0, :, cs], kc_ref[0, 0, :, cs], kn_ref[0, 0, :, cs]], axis=0)
        vwin = jnp.concatenate([vp_ref[0, 0, :, cs], vc_ref[0, 0, :, cs], vn_ref[0, 0, :, cs]], axis=0)
        s = lax.dot_general(q, kwin, (((1,), (1,)), ((), ())), preferred_element_type=F32)
        s = s * scale + (bias_ref[hh] + edge)
        m = jnp.max(s, axis=-1, keepdims=True)
        p = jnp.exp(s - m)
        l = jnp.sum(p, axis=-1, keepdims=True)
        o = jnp.dot(p.astype(BF16), vwin, preferred_element_type=F32)
        o_ref[0, 0, :, cs] = o / l
        lz_all = jnp.where(lane_head == hh, m + jnp.log(l), lz_all)
    lz_ref[0, 0] = lz_all


def _attn_b_group(qkv, rel_bias, g, dil):
    batch, _, ln, _ = qkv.shape
    tq = min(B_TQ, ln)
    assert ln % tq == 0 and tq % B_HALO == 0
    n_tiles = ln // tq
    hpt = tq // B_HALO
    halo_blocks = ln // B_HALO
    w = tq + 2 * B_HALO
    bias, edges = _b_bias_tables(rel_bias, g, dil, tq)

    def cur(which):
        return pl.BlockSpec((1, 1, tq, B_GROUP_WIDTH), lambda b, c, i: (b, c, i, which))

    def prev(which):
        return pl.BlockSpec((1, 1, B_HALO, B_GROUP_WIDTH),
                            lambda b, c, i: (b, c, jnp.maximum(i * hpt - 1, 0), which))

    def nxt(which):
        return pl.BlockSpec((1, 1, B_HALO, B_GROUP_WIDTH),
                            lambda b, c, i: (b, c, jnp.minimum((i + 1) * hpt, halo_blocks - 1), which))

    def edge_row(i):
        return (i == 0).astype(jnp.int32) + 2 * (i == n_tiles - 1).astype(jnp.int32)

    return pl.pallas_call(
        _attn_b_kernel,
        grid=(batch, dil, n_tiles),
        in_specs=[
            cur(0), prev(1), cur(1), nxt(1), prev(2), cur(2), nxt(2),
            pl.BlockSpec((B_HEADS_PER_GROUP, tq, w), lambda b, c, i: (0, 0, 0)),
            pl.BlockSpec((1, 1, w), lambda b, c, i: (edge_row(i), 0, 0)),
        ],
        out_specs=[
            pl.BlockSpec((1, 1, tq, B_GROUP_WIDTH), lambda b, c, i: (b, c, i, 0)),
            pl.BlockSpec((1, 1, tq, LANES), lambda b, c, i: (b, c, i, 0)),
        ],
        out_shape=[
            jax.ShapeDtypeStruct((batch, dil, ln, B_GROUP_WIDTH), F32),
            jax.ShapeDtypeStruct((batch, dil, ln, LANES), F32),
        ],
        compiler_params=_cparams("parallel", "parallel", "parallel"),
        name=f"attn_b_g{g}",
    )(qkv, qkv, qkv, qkv, qkv, qkv, qkv, bias, edges)


def _mix_proj_kernel(o0_ref, o1_ref, o2_ref, z0_ref, z1_ref, z2_ref, w_ref, r_ref, out_ref,
                     y_ref, ot_ref, zt_ref):
    tm = out_ref.shape[0]

    def to_token_order(src_ref, dst_ref, dil):
        rows = tm // dil
        for c in range(dil):
            for k in range(dst_ref.shape[0]):
                dst_ref[k, pl.ds(c, rows, stride=dil), :] = src_ref[0, c, :, k * LANES:(k + 1) * LANES]

    @pl.when(pl.program_id(1) == 0)
    def _():
        z = []
        for g, z_ref in enumerate((z0_ref, z1_ref, z2_ref)):
            dil = B_GROUPS[g][1]
            if dil == 1:
                z.append(z_ref[0, 0])
            else:
                to_token_order(z_ref, zt_ref.at[g:g + 1], dil)
                z.append(zt_ref[g])
        zmax = functools.reduce(jnp.maximum, z)
        e = [jnp.exp(zz - zmax) for zz in z]
        den = functools.reduce(lambda a, b: a + b, e)
        lanes_per_head = LANES // B_HEADS_PER_GROUP
        for g, o_ref in enumerate((o0_ref, o1_ref, o2_ref)):
            dil = B_GROUPS[g][1]
            wt = e[g] / den
            if dil > 1:
                to_token_order(o_ref, ot_ref, dil)
            for hh in range(B_HEADS_PER_GROUP):
                wcol = wt[:, hh * lanes_per_head:hh * lanes_per_head + 1]
                cs = slice(hh * HEAD_DIM, (hh + 1) * HEAD_DIM)
                og = o_ref[0, 0, :, cs] if dil == 1 else ot_ref[hh]
                y_ref[:, g * B_GROUP_WIDTH + hh * HEAD_DIM:g * B_GROUP_WIDTH + (hh + 1) * HEAD_DIM] = (
                    wcol * og).astype(BF16)

    out_ref[...] = r_ref[...] + jnp.dot(y_ref[...], w_ref[...], preferred_element_type=F32)


def _mix_proj(outs, lzs, w, resid, seq, tm, tn):
    m = resid.shape[0]
    k, n = w.shape
    tm, tn = min(tm, seq), min(tn, n)
    assert seq % tm == 0 and n % tn == 0
    tiles_per_seq = seq // tm

    def sub_major(width, dil):
        assert tm % (dil * 8) == 0
        return pl.BlockSpec((1, dil, tm // dil, width),
                            lambda i, j: (i // tiles_per_seq, 0, i % tiles_per_seq, 0))

    dils = [dil for _, dil in B_GROUPS]
    return pl.pallas_call(
        _mix_proj_kernel,
        grid=(m // tm, n // tn),
        in_specs=[sub_major(B_GROUP_WIDTH, dil) for dil in dils] + [sub_major(LANES, dil) for dil in dils] + [
            pl.BlockSpec((k, tn), lambda i, j: (0, j)),
            pl.BlockSpec((tm, tn), lambda i, j: (i, j))],
        out_specs=pl.BlockSpec((tm, tn), lambda i, j: (i, j)),
        out_shape=jax.ShapeDtypeStruct((m, n), F32),
        scratch_shapes=[pltpu.VMEM((tm, k), BF16),
                        pltpu.VMEM((B_HEADS_PER_GROUP, tm, HEAD_DIM), F32),
                        pltpu.VMEM((len(dils), tm, LANES), F32)],
        compiler_params=_cparams("parallel", "arbitrary"),
        name="mix_proj_b",
    )(*outs, *lzs, w, resid)


def _ffn_up_kernel(xp_ref, x_ref, xn_ref, g_ref, wg_ref, wv_ref, cwg_ref, cwv_ref, cbg_ref, cbv_ref,
                   o_ref, xs_ref, *, seq):
    i = pl.program_id(0)
    tm = x_ref.shape[0]
    hl = CONV_HALO
    ext = tm + 2 * hl

    @pl.when(pl.program_id(1) == 0)
    def _():
        gain = g_ref[...]
        at_start = (i * tm) % seq == 0
        at_end = ((i + 1) * tm) % seq == 0
        xs_ref[0:hl, :] = jnp.where(at_start, 0.0, _rms(xp_ref[...], gain)).astype(BF16)
        xs_ref[hl:hl + tm, :] = _rms(x_ref[...], gain).astype(BF16)
        xs_ref[hl + tm:ext, :] = jnp.where(at_end, 0.0, _rms(xn_ref[...], gain)).astype(BF16)

    xs = xs_ref[...]

    def conv(w_ref, cw_ref, cb_ref):
        u = jnp.dot(xs, w_ref[...], preferred_element_type=F32)
        below = pltpu.roll(u, 1, 0)[hl:hl + tm]
        above = pltpu.roll(u, ext - 1, 0)[hl:hl + tm]
        cw = cw_ref[...]
        return below * cw[0:1] + u[hl:hl + tm] * cw[1:2] + above * cw[2:3] + cb_ref[...]

    gate = conv(wg_ref, cwg_ref, cbg_ref)
    val = conv(wv_ref, cwv_ref, cbv_ref)
    o_ref[...] = (gate * jax.nn.sigmoid(gate) * val).astype(BF16)


def _ffn_up(h, gain, w_up, conv_w, conv_b, seq, tm, tn):
    m, d = h.shape
    dff = w_up.shape[1] // 2
    tm, tn = min(tm, seq), min(tn, dff)
    hl = CONV_HALO
    assert m % tm == 0 and seq % tm == 0 and dff % tn == 0 and tm % hl == 0
    nj = dff // tn
    rpt = tm // hl
    last_halo = m // hl - 1
    return pl.pallas_call(
        functools.partial(_ffn_up_kernel, seq=seq),
        grid=(m // tm, nj),
        in_specs=[
            pl.BlockSpec((hl, d), lambda i, j: (jnp.maximum(i * rpt - 1, 0), 0)),
            pl.BlockSpec((tm, d), lambda i, j: (i, 0)),
            pl.BlockSpec((hl, d), lambda i, j: (jnp.minimum((i + 1) * rpt, last_halo), 0)),
            pl.BlockSpec((1, d), lambda i, j: (0, 0)),
            pl.BlockSpec((d, tn), lambda i, j: (0, j)),
            pl.BlockSpec((d, tn), lambda i, j: (0, nj + j)),
            pl.BlockSpec((3, tn), lambda i, j: (0, j)),
            pl.BlockSpec((3, tn), lambda i, j: (0, nj + j)),
            pl.BlockSpec((1, tn), lambda i, j: (0, j)),
            pl.BlockSpec((1, tn), lambda i, j: (0, nj + j)),
        ],
        out_specs=pl.BlockSpec((tm, tn), lambda i, j: (i, j)),
        out_shape=jax.ShapeDtypeStruct((m, dff), BF16),
        scratch_shapes=[pltpu.VMEM((tm + 2 * hl, d), BF16)],
        compiler_params=_cparams("parallel", "arbitrary"),
        name="ffn_up",
    )(h, h, h, gain, w_up, w_up, conv_w, conv_w, conv_b, conv_b)


def _final_norm_kernel(x_ref, g_ref, o_ref):
    o_ref[...] = _rms(x_ref[...], g_ref[...])


def _final_norm(h, gain, tm):
    m, d = h.shape
    tm = min(tm, m)
    assert m % tm == 0
    return pl.pallas_call(
        _final_norm_kernel,
        grid=(m // tm,),
        in_specs=[pl.BlockSpec((tm, d), lambda i: (i, 0)), pl.BlockSpec((1, d), lambda i: (0, 0))],
        out_specs=pl.BlockSpec((tm, d), lambda i: (i, 0)),
        out_shape=jax.ShapeDtypeStruct((m, d), F32),
        compiler_params=_cparams("parallel"),
        name="final_norm",
    )(h, gain)


def _rope_tables(seq):
    rows = seq // GRID_W
    row_ids = jnp.repeat(jnp.arange(rows, dtype=F32), GRID_W)
    col_ids = jnp.tile(jnp.arange(GRID_W, dtype=F32), rows)
    quarter = HEAD_DIM // 4
    inv_freq = ROPE_THETA ** (-jnp.arange(quarter, dtype=F32) / quarter)
    ang_r = row_ids[:, None] * inv_freq[None, :]
    ang_c = col_ids[:, None] * inv_freq[None, :]
    cos_t = jnp.concatenate([jnp.cos(ang_r), jnp.cos(ang_r), jnp.cos(ang_c), jnp.cos(ang_c)], axis=-1)
    sin_t = jnp.concatenate([-jnp.sin(ang_r), jnp.sin(ang_r), -jnp.sin(ang_c), jnp.sin(ang_c)], axis=-1)
    return cos_t, sin_t


def kernel(x, a_w_qkv, a_w_o, a_q_gain, a_k_gain, b_w_qkv, b_w_o, rel_bias, mix_norm, ffn_norm, w_up, conv_w,
           conv_b, w_down, final_norm):
    batch, seq, d = x.shape
    m = batch * seq
    depth = mix_norm.shape[0]
    h = x.reshape(m, d)
    cos_t, sin_t = _rope_tables(seq)
    for i in range(depth):
        jj = i // 2
        gain = mix_norm[i][None, :]
        if i % 2 == 0:
            qkv = _qkv_a(h, gain, a_w_qkv[jj].astype(BF16), cos_t, sin_t,
                         a_q_gain[jj][None, :], a_k_gain[jj][None, :], seq)
            o = _attn_a(qkv, batch, seq)
            h = _matmul_resid(o, a_w_o[jj].astype(BF16), h, TM, TN, "out_proj_a")
        else:
            w_qkv = b_w_qkv[jj].astype(BF16)
            outs, lzs = [], []
            for g, (_, dil) in enumerate(B_GROUPS):
                qkv = _qkv_b_group(h, gain, w_qkv, g, dil, batch, seq, TM // 2, 2 * TN)
                o, lz = _attn_b_group(qkv, rel_bias, g, dil)
                outs.append(o)
                lzs.append(lz)
            h = _mix_proj(outs, lzs, b_w_o[jj].astype(BF16), h, seq, TM // 2, TN)
        act = _ffn_up(h, ffn_norm[i][None, :], w_up[i].astype(BF16), conv_w[i], conv_b[i][None, :], seq, TM, TN)
        h = _matmul_resid(act, w_down[i].astype(BF16), h, TM // 2, TN, "ffn_down")
    return _final_norm(h, final_norm[None, :], TM // 2).reshape(batch, seq, d)
```

```python
import functools
import math

import jax
import jax.numpy as jnp
from jax import lax
from jax.experimental import pallas as pl
from jax.experimental.pallas import tpu as pltpu

GRID_W = 64
HEAD_DIM = 128
A_HEADS = 16
A_KV_HEADS = 4
A_GROUP = A_HEADS // A_KV_HEADS
ROPE_THETA = 10000.0
B_GROUPS = ((128, 1), (512, 4), (2048, 16))
B_HEADS_PER_GROUP = 8
B_GROUP_WIDTH = B_HEADS_PER_GROUP * HEAD_DIM
REL_BUCKETS = 32
REL_MAX_DISTANCE = 1024
EPS = 1e-6
NEG_INF = -1e30
LOG2E = 1.4426950408889634

V7X_VMEM_LIMIT_BYTES = 56 * 1024 * 1024
BF16_SUBLANES = 16
LANES = 128

TM = 1024
TN = 512
A_TQ = 256
A_TK = 512
A_UNROLL = 2
B_TQ = 128
B_HALO = 64
CONV_HALO = BF16_SUBLANES

F32 = jnp.float32
BF16 = jnp.bfloat16


def _cparams(*sem):
    return pltpu.CompilerParams(dimension_semantics=sem, vmem_limit_bytes=V7X_VMEM_LIMIT_BYTES)


def _rms(x, gain):
    ms = jnp.mean(x * x, axis=-1, keepdims=True)
    return x * lax.rsqrt(ms + EPS) * gain


def _qkv_a_kernel(x_ref, g_ref, w_ref, cos_ref, sin_ref, qg_ref, kg_ref, o_ref, xn_ref, *, n_qk_tiles, n_q_tiles):
    j = pl.program_id(1)

    @pl.when(j == 0)
    def _():
        xn_ref[...] = _rms(x_ref[...], g_ref[...]).astype(BF16)

    y = jnp.dot(xn_ref[...], w_ref[...], preferred_element_type=F32)

    @pl.when(j < n_qk_tiles)
    def _():
        gain = jnp.where(j < n_q_tiles, qg_ref[...], kg_ref[...])
        cos = cos_ref[...]
        sin = sin_ref[...]
        lane = lax.broadcasted_iota(jnp.int32, (1, HEAD_DIM), 1)
        first_quarter = (lane % (HEAD_DIM // 2)) < (HEAD_DIM // 4)
        for hh in range(y.shape[1] // HEAD_DIM):
            yh = y[:, hh * HEAD_DIM:(hh + 1) * HEAD_DIM]
            yn = _rms(yh, gain)
            partner = jnp.where(first_quarter,
                                pltpu.roll(yn, HEAD_DIM - HEAD_DIM // 4, 1),
                                pltpu.roll(yn, HEAD_DIM // 4, 1))
            o_ref[:, hh * HEAD_DIM:(hh + 1) * HEAD_DIM] = (yn * cos + partner * sin).astype(BF16)

    @pl.when(j >= n_qk_tiles)
    def _():
        o_ref[...] = y.astype(BF16)


def _qkv_a(h, gain, w, cos_t, sin_t, q_gain, k_gain, seq):
    m, d = h.shape
    n = w.shape[1]
    tm, tn = min(TM, seq), TN
    nq = A_HEADS * HEAD_DIM
    nk = A_KV_HEADS * HEAD_DIM
    assert m % tm == 0 and seq % tm == 0 and n % tn == 0 and nq % tn == 0 and (nq + nk) % tn == 0
    pos_blocks = seq // tm
    kern = functools.partial(_qkv_a_kernel, n_qk_tiles=(nq + nk) // tn, n_q_tiles=nq // tn)
    return pl.pallas_call(
        kern,
        grid=(m // tm, n // tn),
        in_specs=[
            pl.BlockSpec((tm, d), lambda i, j: (i, 0)),
            pl.BlockSpec((1, d), lambda i, j: (0, 0)),
            pl.BlockSpec((d, tn), lambda i, j: (0, j)),
            pl.BlockSpec((tm, HEAD_DIM), lambda i, j: (i % pos_blocks, 0)),
            pl.BlockSpec((tm, HEAD_DIM), lambda i, j: (i % pos_blocks, 0)),
            pl.BlockSpec((1, HEAD_DIM), lambda i, j: (0, 0)),
            pl.BlockSpec((1, HEAD_DIM), lambda i, j: (0, 0)),
        ],
        out_specs=pl.BlockSpec((tm, tn), lambda i, j: (i, j)),
        out_shape=jax.ShapeDtypeStruct((m, n), BF16),
        scratch_shapes=[pltpu.VMEM((tm, d), BF16)],
        compiler_params=_cparams("parallel", "arbitrary"),
        name="qkv_a",
    )(h, gain, w, cos_t, sin_t, q_gain, k_gain)


def _attn_a_kernel(qt_ref, k_ref, vt_ref, o_ref, acc_ref, s_ref, *, tk):
    tq = qt_ref.shape[1]
    seq = k_ref.shape[0]
    n_chunks = seq // tk
    c = (HEAD_DIM ** -0.5) * LOG2E
    acc_ref[...] = jnp.zeros_like(acc_ref)

    def scores(t, g, slot):
        start = pl.multiple_of(t * tk, tk)
        s_ref[slot, g] = jnp.dot(k_ref[pl.ds(start, tk), :], qt_ref[g * HEAD_DIM:(g + 1) * HEAD_DIM, :],
                                 preferred_element_type=F32)

    def chunk(t, t_next, slot, ms, ls):
        start = pl.multiple_of(t * tk, tk)
        vt = vt_ref[:, pl.ds(start, tk)]
        new_ms, new_ls = [], []
        for g in range(A_GROUP):
            scores(t_next, g, 1 - slot)
            s = s_ref[slot, g]
            m_new = jnp.maximum(ms[g], jnp.max(s, axis=0, keepdims=True))
            alpha = jnp.exp2((ms[g] - m_new) * c)
            p = jnp.exp2((s - m_new) * c)
            new_ls.append(alpha * ls[g] + jnp.sum(p, axis=0, keepdims=True))
            new_ms.append(m_new)
            cols = slice(g * tq, (g + 1) * tq)
            acc_ref[:, cols] = alpha * acc_ref[:, cols] + jnp.dot(vt, p.astype(BF16), preferred_element_type=F32)
        return tuple(new_ms), tuple(new_ls)

    def body(i, carry):
        ms, ls = carry
        for u in range(A_UNROLL):
            t = A_UNROLL * i + u
            ms, ls = chunk(t, jnp.minimum(t + 1, n_chunks - 1), u % 2, ms, ls)
        return ms, ls

    for g in range(A_GROUP):
        scores(0, g, 0)
    m0 = tuple(jnp.full((1, tq), NEG_INF, F32) for _ in range(A_GROUP))
    l0 = tuple(jnp.zeros((1, tq), F32) for _ in range(A_GROUP))
    _, ls = lax.fori_loop(0, n_chunks // A_UNROLL, body, (m0, l0))
    for g in range(A_GROUP):
        out_t = acc_ref[:, g * tq:(g + 1) * tq] / ls[g]
        o_ref[:, g * HEAD_DIM:(g + 1) * HEAD_DIM] = out_t.T.astype(BF16)


def _attn_a(qkv, batch, seq):
    m = qkv.shape[0]
    tq = min(A_TQ, seq)
    tk = min(A_TK, seq)
    assert seq % tq == 0 and seq % (A_UNROLL * tk) == 0 and A_UNROLL % 2 == 0
    nq = A_HEADS * HEAD_DIM
    nk = A_KV_HEADS * HEAD_DIM
    gw = A_GROUP * HEAD_DIM
    q_blocks = seq // tq
    q_t = qkv[:, :nq].T
    v_t = qkv[:, nq + nk:].T
    return pl.pallas_call(
        functools.partial(_attn_a_kernel, tk=tk),
        grid=(batch, A_KV_HEADS, q_blocks),
        in_specs=[
            pl.BlockSpec((gw, tq), lambda b, kv, qi: (kv, b * q_blocks + qi)),
            pl.BlockSpec((seq, HEAD_DIM), lambda b, kv, qi: (b, A_HEADS + kv)),
            pl.BlockSpec((HEAD_DIM, seq), lambda b, kv, qi: (kv, b)),
        ],
        out_specs=pl.BlockSpec((tq, gw), lambda b, kv, qi: (b * q_blocks + qi, kv)),
        out_shape=jax.ShapeDtypeStruct((m, nq), BF16),
        scratch_shapes=[pltpu.VMEM((HEAD_DIM, A_GROUP * tq), F32),
                        pltpu.VMEM((2, A_GROUP, tk, tq), F32)],
        compiler_params=_cparams("parallel", "parallel", "arbitrary"),
        name="attn_a",
    )(q_t, qkv, v_t)


def _matmul_resid_kernel(x_ref, w_ref, r_ref, o_ref):
    o_ref[...] = r_ref[...] + jnp.dot(x_ref[...], w_ref[...], preferred_element_type=F32)


def _matmul_resid(x, w, resid, tm, tn, name):
    m, k = x.shape
    n = w.shape[1]
    tm, tn = min(tm, m), min(tn, n)
    assert m % tm == 0 and n % tn == 0
    return pl.pallas_call(
        _matmul_resid_kernel,
        grid=(m // tm, n // tn),
        in_specs=[
            pl.BlockSpec((tm, k), lambda i, j: (i, 0)),
            pl.BlockSpec((k, tn), lambda i, j: (0, j)),
            pl.BlockSpec((tm, tn), lambda i, j: (i, j)),
        ],
        out_specs=pl.BlockSpec((tm, tn), lambda i, j: (i, j)),
        out_shape=jax.ShapeDtypeStruct((m, n), F32),
        compiler_params=_cparams("parallel", "parallel"),
        name=name,
    )(x, w, resid)


def _qkv_b_kernel(x_ref, g_ref, w_ref, o_ref, xs_ref, *lane_chunks, dil):
    tm, d = x_ref.shape
    n = tm // dil

    @pl.when(pl.program_id(1) == 0)
    def _():
        gain = g_ref[...]
        if dil == 1:
            xs_ref[...] = _rms(x_ref[...], gain).astype(BF16)
        else:
            xc_ref, = lane_chunks
            for k in range(d // LANES):
                xc_ref[k] = x_ref[:, k * LANES:(k + 1) * LANES]
            for c in range(dil):
                rows = jnp.concatenate(
                    [xc_ref[k, pl.ds(c, n, stride=dil), :] for k in range(d // LANES)], axis=1)
                xs_ref[c * n:(c + 1) * n, :] = _rms(rows, gain).astype(BF16)

    y = jnp.dot(xs_ref[...], w_ref[...], preferred_element_type=F32)
    for c in range(dil):
        o_ref[0, c] = y[c * n:(c + 1) * n].astype(BF16)


def _qkv_b_group(h, gain, w, g, dil, batch, seq, tm, tn):
    m, d = h.shape
    gw = 3 * B_GROUP_WIDTH
    tm = min(tm, seq)
    assert seq % tm == 0 and gw % tn == 0 and tm % (dil * BF16_SUBLANES) == 0
    tiles_per_seq = seq // tm
    n = tm // dil
    col0 = g * (gw // tn)
    return pl.pallas_call(
        functools.partial(_qkv_b_kernel, dil=dil),
        grid=(m // tm, gw // tn),
        in_specs=[
            pl.BlockSpec((tm, d), lambda i, j: (i, 0)),
            pl.BlockSpec((1, d), lambda i, j: (0, 0)),
            pl.BlockSpec((d, tn), lambda i, j: (0, col0 + j)),
        ],
        out_specs=pl.BlockSpec((1, dil, n, tn), lambda i, j: (i // tiles_per_seq, 0, i % tiles_per_seq, j)),
        out_shape=jax.ShapeDtypeStruct((batch, dil, seq // dil, gw), BF16),
        scratch_shapes=[pltpu.VMEM((tm, d), BF16)] + (
            [pltpu.VMEM((d // LANES, tm, LANES), F32)] if dil > 1 else []),
        compiler_params=_cparams("parallel", "arbitrary"),
        name=f"qkv_b_g{g}",
    )(h, gain, w)


def _t5_bucket(rel):
    nb = REL_BUCKETS // 2
    max_exact = nb // 2
    base = jnp.where(rel > 0, nb, 0)
    n = jnp.abs(rel)
    nf = jnp.maximum(n, 1).astype(F32)
    large = max_exact + (jnp.log(nf / max_exact) / math.log(REL_MAX_DISTANCE / max_exact)
                         * (nb - max_exact)).astype(jnp.int32)
    large = jnp.minimum(large, nb - 1)
    return base + jnp.where(n < max_exact, n, large)


def _b_bias_tables(rel_bias, g, dil, tq):
    w = tq + 2 * B_HALO
    kk = jnp.arange(w)
    rel = kk[None, :] - B_HALO - jnp.arange(tq)[:, None]
    hg = B_HEADS_PER_GROUP
    bias = rel_bias[:, g * hg:(g + 1) * hg][_t5_bucket(rel * dil)].astype(F32).transpose(2, 0, 1)
    bias = jnp.where((jnp.abs(rel) <= B_HALO)[None], bias, NEG_INF)
    edges = []
    for first, last in ((False, False), (True, False), (False, True), (True, True)):
        off = jnp.zeros((w,), bool)
        if first:
            off = off | (kk < B_HALO)
        if last:
            off = off | (kk >= tq + B_HALO)
        edges.append(jnp.where(off, NEG_INF, 0.0).astype(F32)[None, :])
    return bias, jnp.stack(edges)


def _attn_b_kernel(q_ref, kp_ref, kc_ref, kn_ref, vp_ref, vc_ref, vn_ref, bias_ref, edge_ref, o_ref, lz_ref):
    tq = q_ref.shape[2]
    scale = HEAD_DIM ** -0.5
    lane_head = lax.broadcasted_iota(jnp.int32, (1, LANES), 1) // (LANES // B_HEADS_PER_GROUP)
    lz_all = jnp.zeros((tq, LANES), F32)
    edge = edge_ref[0]
    for hh in range(B_HEADS_PER_GROUP):
        cs = slice(hh * HEAD_DIM, (hh + 1) * HEAD_DIM)
        q = q_ref[0, 0, :, cs]
        kwin = jnp.concatenate([kp_ref[0, 0, :, cs], kc_ref[0, 0, :, cs], kn_ref[0, 0, :, cs]], axis=0)
        vwin = jnp.concatenate([vp_ref[0, 0, :, cs], vc_ref[0, 0, :, cs], vn_ref[0, 0, :, cs]], axis=0)
        s = lax.dot_general(q, kwin, (((1,), (1,)), ((), ())), preferred_element_type=F32)
        s = s * scale + (bias_ref[hh] + edge)
        m = jnp.max(s, axis=-1, keepdims=True)
        p = jnp.exp(s - m)
        l = jnp.sum(p, axis=-1, keepdims=True)
        o = jnp.dot(p.astype(BF16), vwin, preferred_element_type=F32)
        o_ref[0, 0, :, cs] = o / l
        lz_all = jnp.where(lane_head == hh, m + jnp.log(l), lz_all)
    lz_ref[0, 0] = lz_all


def _attn_b_group(qkv, rel_bias, g, dil):
    batch, _, ln, _ = qkv.shape
    tq = min(B_TQ, ln)
    assert ln % tq == 0 and tq % B_HALO == 0
    n_tiles = ln // tq
    hpt = tq // B_HALO
    halo_blocks = ln // B_HALO
    w = tq + 2 * B_HALO
    bias, edges = _b_bias_tables(rel_bias, g, dil, tq)

    def cur(which):
        return pl.BlockSpec((1, 1, tq, B_GROUP_WIDTH), lambda b, c, i: (b, c, i, which))

    def prev(which):
        return pl.BlockSpec((1, 1, B_HALO, B_GROUP_WIDTH),
                            lambda b, c, i: (b, c, jnp.maximum(i * hpt - 1, 0), which))

    def nxt(which):
        return pl.BlockSpec((1, 1, B_HALO, B_GROUP_WIDTH),
                            lambda b, c, i: (b, c, jnp.minimum((i + 1) * hpt, halo_blocks - 1), which))

    def edge_row(i):
        return (i == 0).astype(jnp.int32) + 2 * (i == n_tiles - 1).astype(jnp.int32)

    return pl.pallas_call(
        _attn_b_kernel,
        grid=(batch, dil, n_tiles),
        in_specs=[
            cur(0), prev(1), cur(1), nxt(1), prev(2), cur(2), nxt(2),
            pl.BlockSpec((B_HEADS_PER_GROUP, tq, w), lambda b, c, i: (0, 0, 0)),
            pl.BlockSpec((1, 1, w), lambda b, c, i: (edge_row(i), 0, 0)),
        ],
        out_specs=[
            pl.BlockSpec((1, 1, tq, B_GROUP_WIDTH), lambda b, c, i: (b, c, i, 0)),
            pl.BlockSpec((1, 1, tq, LANES), lambda b, c, i: (b, c, i, 0)),
        ],
        out_shape=[
            jax.ShapeDtypeStruct((batch, dil, ln, B_GROUP_WIDTH), F32),
            jax.ShapeDtypeStruct((batch, dil, ln, LANES), F32),
        ],
        compiler_params=_cparams("parallel", "parallel", "parallel"),
        name=f"attn_b_g{g}",
    )(qkv, qkv, qkv, qkv, qkv, qkv, qkv, bias, edges)


def _mix_proj_kernel(o0_ref, o1_ref, o2_ref, z0_ref, z1_ref, z2_ref, w_ref, r_ref, out_ref,
                     y_ref, ot_ref, zt_ref):
    tm = out_ref.shape[0]

    def to_token_order(src_ref, dst_ref, dil):
        rows = tm // dil
        for c in range(dil):
            for k in range(dst_ref.shape[0]):
                dst_ref[k, pl.ds(c, rows, stride=dil), :] = src_ref[0, c, :, k * LANES:(k + 1) * LANES]

    @pl.when(pl.program_id(1) == 0)
    def _():
        z = []
        for g, z_ref in enumerate((z0_ref, z1_ref, z2_ref)):
            dil = B_GROUPS[g][1]
            if dil == 1:
                z.append(z_ref[0, 0])
            else:
                to_token_order(z_ref, zt_ref.at[g:g + 1], dil)
                z.append(zt_ref[g])
        zmax = functools.reduce(jnp.maximum, z)
        e = [jnp.exp(zz - zmax) for zz in z]
        den = functools.reduce(lambda a, b: a + b, e)
        lanes_per_head = LANES // B_HEADS_PER_GROUP
        for g, o_ref in enumerate((o0_ref, o1_ref, o2_ref)):
            dil = B_GROUPS[g][1]
            wt = e[g] / den
            if dil > 1:
                to_token_order(o_ref, ot_ref, dil)
            for hh in range(B_HEADS_PER_GROUP):
                wcol = wt[:, hh * lanes_per_head:hh * lanes_per_head + 1]
                cs = slice(hh * HEAD_DIM, (hh + 1) * HEAD_DIM)
                og = o_ref[0, 0, :, cs] if dil == 1 else ot_ref[hh]
                y_ref[:, g * B_GROUP_WIDTH + hh * HEAD_DIM:g * B_GROUP_WIDTH + (hh + 1) * HEAD_DIM] = (
                    wcol * og).astype(BF16)

    out_ref[...] = r_ref[...] + jnp.dot(y_ref[...], w_ref[...], preferred_element_type=F32)


def _mix_proj(outs, lzs, w, resid, seq, tm, tn):
    m = resid.shape[0]
    k, n = w.shape
    tm, tn = min(tm, seq), min(tn, n)
    assert seq % tm == 0 and n % tn == 0
    tiles_per_seq = seq // tm

    def sub_major(width, dil):
        assert tm % (dil * 8) == 0
        return pl.BlockSpec((1, dil, tm // dil, width),
                            lambda i, j: (i // tiles_per_seq, 0, i % tiles_per_seq, 0))

    dils = [dil for _, dil in B_GROUPS]
    return pl.pallas_call(
        _mix_proj_kernel,
        grid=(m // tm, n // tn),
        in_specs=[sub_major(B_GROUP_WIDTH, dil) for dil in dils] + [sub_major(LANES, dil) for dil in dils] + [
            pl.BlockSpec((k, tn), lambda i, j: (0, j)),
            pl.BlockSpec((tm, tn), lambda i, j: (i, j))],
        out_specs=pl.BlockSpec((tm, tn), lambda i, j: (i, j)),
        out_shape=jax.ShapeDtypeStruct((m, n), F32),
        scratch_shapes=[pltpu.VMEM((tm, k), BF16),
                        pltpu.VMEM((B_HEADS_PER_GROUP, tm, HEAD_DIM), F32),
                        pltpu.VMEM((len(dils), tm, LANES), F32)],
        compiler_params=_cparams("parallel", "arbitrary"),
        name="mix_proj_b",
    )(*outs, *lzs, w, resid)


def _ffn_up_kernel(xp_ref, x_ref, xn_ref, g_ref, wg_ref, wv_ref, cwg_ref, cwv_ref, cbg_ref, cbv_ref,
                   o_ref, xs_ref, *, seq):
    i = pl.program_id(0)
    tm = x_ref.shape[0]
    hl = CONV_HALO
    ext = tm + 2 * hl

    @pl.when(pl.program_id(1) == 0)
    def _():
        gain = g_ref[...]
        at_start = (i * tm) % seq == 0
        at_end = ((i + 1) * tm) % seq == 0
        xs_ref[0:hl, :] = jnp.where(at_start, 0.0, _rms(xp_ref[...], gain)).astype(BF16)
        xs_ref[hl:hl + tm, :] = _rms(x_ref[...], gain).astype(BF16)
        xs_ref[hl + tm:ext, :] = jnp.where(at_end, 0.0, _rms(xn_ref[...], gain)).astype(BF16)

    xs = xs_ref[...]

    def conv(w_ref, cw_ref, cb_ref):
        u = jnp.dot(xs, w_ref[...], preferred_element_type=F32)
        below = pltpu.roll(u, 1, 0)[hl:hl + tm]
        above = pltpu.roll(u, ext - 1, 0)[hl:hl + tm]
        cw = cw_ref[...]
        return below * cw[0:1] + u[hl:hl + tm] * cw[1:2] + above * cw[2:3] + cb_ref[...]

    gate = conv(wg_ref, cwg_ref, cbg_ref)
    val = conv(wv_ref, cwv_ref, cbv_ref)
    o_ref[...] = (gate * jax.nn.sigmoid(gate) * val).astype(BF16)


def _ffn_up(h, gain, w_up, conv_w, conv_b, seq, tm, tn):
    m, d = h.shape
    dff = w_up.shape[1] // 2
    tm, tn = min(tm, seq), min(tn, dff)
    hl = CONV_HALO
    assert m % tm == 0 and seq % tm == 0 and dff % tn == 0 and tm % hl == 0
    nj = dff // tn
    rpt = tm // hl
    last_halo = m // hl - 1
    return pl.pallas_call(
        functools.partial(_ffn_up_kernel, seq=seq),
        grid=(m // tm, nj),
        in_specs=[
            pl.BlockSpec((hl, d), lambda i, j: (jnp.maximum(i * rpt - 1, 0), 0)),
            pl.BlockSpec((tm, d), lambda i, j: (i, 0)),
            pl.BlockSpec((hl, d), lambda i, j: (jnp.minimum((i + 1) * rpt, last_halo), 0)),
            pl.BlockSpec((1, d), lambda i, j: (0, 0)),
            pl.BlockSpec((d, tn), lambda i, j: (0, j)),
            pl.BlockSpec((d, tn), lambda i, j: (0, nj + j)),
            pl.BlockSpec((3, tn), lambda i, j: (0, j)),
            pl.BlockSpec((3, tn), lambda i, j: (0, nj + j)),
            pl.BlockSpec((1, tn), lambda i, j: (0, j)),
            pl.BlockSpec((1, tn), lambda i, j: (0, nj + j)),
        ],
        out_specs=pl.BlockSpec((tm, tn), lambda i, j: (i, j)),
        out_shape=jax.ShapeDtypeStruct((m, dff), BF16),
        scratch_shapes=[pltpu.VMEM((tm + 2 * hl, d), BF16)],
        compiler_params=_cparams("parallel", "arbitrary"),
        name="ffn_up",
    )(h, h, h, gain, w_up, w_up, conv_w, conv_w, conv_b, conv_b)


def _final_norm_kernel(x_ref, g_ref, o_ref):
    o_ref[...] = _rms(x_ref[...], g_ref[...])


def _final_norm(h, gain, tm):
    m, d = h.shape
    tm = min(tm, m)
    assert m % tm == 0
    return pl.pallas_call(
        _final_norm_kernel,
        grid=(m // tm,),
        in_specs=[pl.BlockSpec((tm, d), lambda i: (i, 0)), pl.BlockSpec((1, d), lambda i: (0, 0))],
        out_specs=pl.BlockSpec((tm, d), lambda i: (i, 0)),
        out_shape=jax.ShapeDtypeStruct((m, d), F32),
        compiler_params=_cparams("parallel"),
        name="final_norm",
    )(h, gain)


def _rope_tables(seq):
    rows = seq // GRID_W
    row_ids = jnp.repeat(jnp.arange(rows, dtype=F32), GRID_W)
    col_ids = jnp.tile(jnp.arange(GRID_W, dtype=F32), rows)
    quarter = HEAD_DIM // 4
    inv_freq = ROPE_THETA ** (-jnp.arange(quarter, dtype=F32) / quarter)
    ang_r = row_ids[:, None] * inv_freq[None, :]
    ang_c = col_ids[:, None] * inv_freq[None, :]
    cos_t = jnp.concatenate([jnp.cos(ang_r), jnp.cos(ang_r), jnp.cos(ang_c), jnp.cos(ang_c)], axis=-1)
    sin_t = jnp.concatenate([-jnp.sin(ang_r), jnp.sin(ang_r), -jnp.sin(ang_c), jnp.sin(ang_c)], axis=-1)
    return cos_t, sin_t


def kernel(x, a_w_qkv, a_w_o, a_q_gain, a_k_gain, b_w_qkv, b_w_o, rel_bias, mix_norm, ffn_norm, w_up, conv_w,
           conv_b, w_down, final_norm):
    batch, seq, d = x.shape
    m = batch * seq
    depth = mix_norm.shape[0]
    h = x.reshape(m, d)
    cos_t, sin_t = _rope_tables(seq)
    for i in range(depth):
        jj = i // 2
        gain = mix_norm[i][None, :]
        if i % 2 == 0:
            qkv = _qkv_a(h, gain, a_w_qkv[jj].astype(BF16), cos_t, sin_t,
                         a_q_gain[jj][None, :], a_k_gain[jj][None, :], seq)
            o = _attn_a(qkv, batch, seq)
            h = _matmul_resid(o, a_w_o[jj].astype(BF16), h, TM, TN, "out_proj_a")
        else:
            w_qkv = b_w_qkv[jj].astype(BF16)
            outs, lzs = [], []
            for g, (_, dil) in enumerate(B_GROUPS):
                qkv = _qkv_b_group(h, gain, w_qkv, g, dil, batch, seq, TM // 2, 2 * TN)
                o, lz = _attn_b_group(qkv, rel_bias, g, dil)
                outs.append(o)
                lzs.append(lz)
            h = _mix_proj(outs, lzs, b_w_o[jj].astype(BF16), h, seq, TM // 2, TN)
        act = _ffn_up(h, ffn_norm[i][None, :], w_up[i].astype(BF16), conv_w[i], conv_b[i][None, :], seq, TM, TN)
        h = _matmul_resid(act, w_down[i].astype(BF16), h, TM // 2, TN, "ffn_down")
    return _final_norm(h, final_norm[None, :], TM // 2).reshape(batch, seq, d)
```

```python
import functools
import math

import jax
import jax.numpy as jnp
from jax import lax
from jax.experimental import pallas as pl
from jax.experimental.pallas import tpu as pltpu

GRID_W = 64
HEAD_DIM = 128
A_HEADS = 16
A_KV_HEADS = 4
A_GROUP = A_HEADS // A_KV_HEADS
ROPE_THETA = 10000.0
B_GROUPS = ((128, 1), (512, 4), (2048, 16))
B_HEADS_PER_GROUP = 8
B_GROUP_WIDTH = B_HEADS_PER_GROUP * HEAD_DIM
REL_BUCKETS = 32
REL_MAX_DISTANCE = 1024
EPS = 1e-6
NEG_INF = -1e30
LOG2E = 1.4426950408889634

V7X_VMEM_LIMIT_BYTES = 56 * 1024 * 1024
BF16_SUBLANES = 16
LANES = 128

TM = 1024
TN = 512
A_TQ = 256
A_TK = 512
A_UNROLL = 2
B_TQ = 128
B_HALO = 64
CONV_HALO = BF16_SUBLANES

F32 = jnp.float32
BF16 = jnp.bfloat16


def _cparams(*sem):
    return pltpu.CompilerParams(dimension_semantics=sem, vmem_limit_bytes=V7X_VMEM_LIMIT_BYTES)


def _rms(x, gain):
    ms = jnp.mean(x * x, axis=-1, keepdims=True)
    return x * lax.rsqrt(ms + EPS) * gain


def _qkv_a_kernel(x_ref, g_ref, w_ref, cos_ref, sin_ref, qg_ref, kg_ref, o_ref, xn_ref, *, n_qk_tiles, n_q_tiles):
    j = pl.program_id(1)

    @pl.when(j == 0)
    def _():
        xn_ref[...] = _rms(x_ref[...], g_ref[...]).astype(BF16)

    y = jnp.dot(xn_ref[...], w_ref[...], preferred_element_type=F32)

    @pl.when(j < n_qk_tiles)
    def _():
        gain = jnp.where(j < n_q_tiles, qg_ref[...], kg_ref[...])
        cos = cos_ref[...]
        sin = sin_ref[...]
        lane = lax.broadcasted_iota(jnp.int32, (1, HEAD_DIM), 1)
        first_quarter = (lane % (HEAD_DIM // 2)) < (HEAD_DIM // 4)
        for hh in range(y.shape[1] // HEAD_DIM):
            yh = y[:, hh * HEAD_DIM:(hh + 1) * HEAD_DIM]
            yn = _rms(yh, gain)
            partner = jnp.where(first_quarter,
                                pltpu.roll(yn, HEAD_DIM - HEAD_DIM // 4, 1),
                                pltpu.roll(yn, HEAD_DIM // 4, 1))
            o_ref[:, hh * HEAD_DIM:(hh + 1) * HEAD_DIM] = (yn * cos + partner * sin).astype(BF16)

    @pl.when(j >= n_qk_tiles)
    def _():
        o_ref[...] = y.astype(BF16)


def _qkv_a(h, gain, w_stack, layer, cos_t, sin_t, q_gain, k_gain, seq):
    m, d = h.shape
    n = w_stack.shape[2]
    tm, tn = min(TM, seq), TN
    nq = A_HEADS * HEAD_DIM
    nk = A_KV_HEADS * HEAD_DIM
    assert m % tm == 0 and seq % tm == 0 and n % tn == 0 and nq % tn == 0 and (nq + nk) % tn == 0
    pos_blocks = seq // tm
    kern = functools.partial(_qkv_a_kernel, n_qk_tiles=(nq + nk) // tn, n_q_tiles=nq // tn)
    return pl.pallas_call(
        kern,
        grid=(m // tm, n // tn),
        in_specs=[
            pl.BlockSpec((tm, d), lambda i, j: (i, 0)),
            pl.BlockSpec((1, d), lambda i, j: (0, 0)),
            pl.BlockSpec((None, d, tn), lambda i, j: (layer, 0, j)),
            pl.BlockSpec((tm, HEAD_DIM), lambda i, j: (i % pos_blocks, 0)),
            pl.BlockSpec((tm, HEAD_DIM), lambda i, j: (i % pos_blocks, 0)),
            pl.BlockSpec((1, HEAD_DIM), lambda i, j: (0, 0)),
            pl.BlockSpec((1, HEAD_DIM), lambda i, j: (0, 0)),
        ],
        out_specs=pl.BlockSpec((tm, tn), lambda i, j: (i, j)),
        out_shape=jax.ShapeDtypeStruct((m, n), BF16),
        scratch_shapes=[pltpu.VMEM((tm, d), BF16)],
        compiler_params=_cparams("parallel", "arbitrary"),
        name="qkv_a",
    )(h, gain, w_stack, cos_t, sin_t, q_gain, k_gain)


def _attn_a_kernel(qt_ref, k_ref, vt_ref, o_ref, acc_ref, s_ref, *, tk):
    tq = qt_ref.shape[1]
    seq = k_ref.shape[0]
    n_chunks = seq // tk
    c = (HEAD_DIM ** -0.5) * LOG2E
    acc_ref[...] = jnp.zeros_like(acc_ref)

    def scores(t, g, slot):
        start = pl.multiple_of(t * tk, tk)
        s_ref[slot, g] = jnp.dot(k_ref[pl.ds(start, tk), :], qt_ref[g * HEAD_DIM:(g + 1) * HEAD_DIM, :],
                                 preferred_element_type=F32)

    def chunk(t, t_next, slot, ms, ls):
        start = pl.multiple_of(t * tk, tk)
        vt = vt_ref[:, pl.ds(start, tk)]
        new_ms, new_ls = [], []
        for g in range(A_GROUP):
            scores(t_next, g, 1 - slot)
            s = s_ref[slot, g]
            m_new = jnp.maximum(ms[g], jnp.max(s, axis=0, keepdims=True))
            alpha = jnp.exp2((ms[g] - m_new) * c)
            p = jnp.exp2((s - m_new) * c)
            new_ls.append(alpha * ls[g] + jnp.sum(p, axis=0, keepdims=True))
            new_ms.append(m_new)
            cols = slice(g * tq, (g + 1) * tq)
            acc_ref[:, cols] = alpha * acc_ref[:, cols] + jnp.dot(vt, p.astype(BF16), preferred_element_type=F32)
        return tuple(new_ms), tuple(new_ls)

    def body(i, carry):
        ms, ls = carry
        for u in range(A_UNROLL):
            t = A_UNROLL * i + u
            ms, ls = chunk(t, jnp.minimum(t + 1, n_chunks - 1), u % 2, ms, ls)
        return ms, ls

    for g in range(A_GROUP):
        scores(0, g, 0)
    m0 = tuple(jnp.full((1, tq), NEG_INF, F32) for _ in range(A_GROUP))
    l0 = tuple(jnp.zeros((1, tq), F32) for _ in range(A_GROUP))
    _, ls = lax.fori_loop(0, n_chunks // A_UNROLL, body, (m0, l0))
    for g in range(A_GROUP):
        out_t = acc_ref[:, g * tq:(g + 1) * tq] / ls[g]
        o_ref[:, g * HEAD_DIM:(g + 1) * HEAD_DIM] = out_t.T.astype(BF16)


def _attn_a(qkv, batch, seq):
    m = qkv.shape[0]
    tq = min(A_TQ, seq)
    tk = min(A_TK, seq)
    assert seq % tq == 0 and seq % (A_UNROLL * tk) == 0 and A_UNROLL % 2 == 0
    nq = A_HEADS * HEAD_DIM
    nk = A_KV_HEADS * HEAD_DIM
    gw = A_GROUP * HEAD_DIM
    q_blocks = seq // tq
    q_t = qkv[:, :nq].T
    v_t = qkv[:, nq + nk:].T
    return pl.pallas_call(
        functools.partial(_attn_a_kernel, tk=tk),
        grid=(batch, A_KV_HEADS, q_blocks),
        in_specs=[
            pl.BlockSpec((gw, tq), lambda b, kv, qi: (kv, b * q_blocks + qi)),
            pl.BlockSpec((seq, HEAD_DIM), lambda b, kv, qi: (b, A_HEADS + kv)),
            pl.BlockSpec((HEAD_DIM, seq), lambda b, kv, qi: (kv, b)),
        ],
        out_specs=pl.BlockSpec((tq, gw), lambda b, kv, qi: (b * q_blocks + qi, kv)),
        out_shape=jax.ShapeDtypeStruct((m, nq), BF16),
        scratch_shapes=[pltpu.VMEM((HEAD_DIM, A_GROUP * tq), F32),
                        pltpu.VMEM((2, A_GROUP, tk, tq), F32)],
        compiler_params=_cparams("parallel", "parallel", "arbitrary"),
        name="attn_a",
    )(q_t, qkv, v_t)


def _matmul_resid_kernel(x_ref, w_ref, r_ref, o_ref):
    o_ref[...] = r_ref[...] + jnp.dot(x_ref[...], w_ref[...], preferred_element_type=F32)


def _matmul_resid(x, w_stack, layer, resid, tm, name):
    m, k = x.shape
    n = w_stack.shape[2]
    tm = min(tm, m)
    assert m % tm == 0
    return pl.pallas_call(
        _matmul_resid_kernel,
        grid=(m // tm,),
        in_specs=[
            pl.BlockSpec((tm, k), lambda i: (i, 0)),
            pl.BlockSpec((None, k, n), lambda i: (layer, 0, 0), pipeline_mode=pl.Buffered(1)),
            pl.BlockSpec((tm, n), lambda i: (i, 0)),
        ],
        out_specs=pl.BlockSpec((tm, n), lambda i: (i, 0)),
        out_shape=jax.ShapeDtypeStruct((m, n), F32),
        compiler_params=_cparams("parallel"),
        name=name,
    )(x, w_stack, resid)


def _qkv_b_kernel(x_ref, g_ref, w_ref, o_ref, xs_ref, *lane_chunks, dil):
    tm, d = x_ref.shape
    n = tm // dil

    @pl.when(pl.program_id(1) == 0)
    def _():
        gain = g_ref[...]
        if dil == 1:
            xs_ref[...] = _rms(x_ref[...], gain).astype(BF16)
        else:
            xc_ref, = lane_chunks
            for k in range(d // LANES):
                xc_ref[k] = x_ref[:, k * LANES:(k + 1) * LANES]
            for c in range(dil):
                rows = jnp.concatenate(
                    [xc_ref[k, pl.ds(c, n, stride=dil), :] for k in range(d // LANES)], axis=1)
                xs_ref[c * n:(c + 1) * n, :] = _rms(rows, gain).astype(BF16)

    y = jnp.dot(xs_ref[...], w_ref[...], preferred_element_type=F32)
    for c in range(dil):
        o_ref[0, c] = y[c * n:(c + 1) * n].astype(BF16)


def _qkv_b_group(h, gain, w_stack, layer, g, dil, batch, seq, tm, tn):
    m, d = h.shape
    gw = 3 * B_GROUP_WIDTH
    tm = min(tm, seq)
    assert seq % tm == 0 and gw % tn == 0 and tm % (dil * BF16_SUBLANES) == 0
    tiles_per_seq = seq // tm
    n = tm // dil
    col0 = g * (gw // tn)
    return pl.pallas_call(
        functools.partial(_qkv_b_kernel, dil=dil),
        grid=(m // tm, gw // tn),
        in_specs=[
            pl.BlockSpec((tm, d), lambda i, j: (i, 0)),
            pl.BlockSpec((1, d), lambda i, j: (0, 0)),
            pl.BlockSpec((None, d, tn), lambda i, j: (layer, 0, col0 + j)),
        ],
        out_specs=pl.BlockSpec((1, dil, n, tn), lambda i, j: (i // tiles_per_seq, 0, i % tiles_per_seq, j)),
        out_shape=jax.ShapeDtypeStruct((batch, dil, seq // dil, gw), BF16),
        scratch_shapes=[pltpu.VMEM((tm, d), BF16)] + (
            [pltpu.VMEM((d // LANES, tm, LANES), F32)] if dil > 1 else []),
        compiler_params=_cparams("parallel", "arbitrary"),
        name=f"qkv_b_g{g}",
    )(h, gain, w_stack)


def _t5_bucket(rel):
    nb = REL_BUCKETS // 2
    max_exact = nb // 2
    base = jnp.where(rel > 0, nb, 0)
    n = jnp.abs(rel)
    nf = jnp.maximum(n, 1).astype(F32)
    large = max_exact + (jnp.log(nf / max_exact) / math.log(REL_MAX_DISTANCE / max_exact)
                         * (nb - max_exact)).astype(jnp.int32)
    large = jnp.minimum(large, nb - 1)
    return base + jnp.where(n < max_exact, n, large)


def _b_bias_tables(rel_bias, g, dil, tq):
    w = tq + 2 * B_HALO
    kk = jnp.arange(w)
    rel = kk[None, :] - B_HALO - jnp.arange(tq)[:, None]
    hg = B_HEADS_PER_GROUP
    bias = rel_bias[:, g * hg:(g + 1) * hg][_t5_bucket(rel * dil)].astype(F32).transpose(2, 0, 1)
    bias = jnp.where((jnp.abs(rel) <= B_HALO)[None], bias, NEG_INF)
    edges = []
    for first, last in ((False, False), (True, False), (False, True), (True, True)):
        off = jnp.zeros((w,), bool)
        if first:
            off = off | (kk < B_HALO)
        if last:
            off = off | (kk >= tq + B_HALO)
        edges.append(jnp.where(off, NEG_INF, 0.0).astype(F32)[None, :])
    return bias, jnp.stack(edges)


def _attn_b_kernel(q_ref, kp_ref, kc_ref, kn_ref, vp_ref, vc_ref, vn_ref, bias_ref, edge_ref, o_ref, lz_ref):
    tq = q_ref.shape[2]
    scale = HEAD_DIM ** -0.5
    lane_head = lax.broadcasted_iota(jnp.int32, (1, LANES), 1) // (LANES // B_HEADS_PER_GROUP)
    lz_all = jnp.zeros((tq, LANES), F32)
    edge = edge_ref[0]
    for hh in range(B_HEADS_PER_GROUP):
        cs = slice(hh * HEAD_DIM, (hh + 1) * HEAD_DIM)
        q = q_ref[0, 0, :, cs]
        kwin = jnp.concatenate([kp_ref[0, 0, :, cs], kc_ref[0, 0, :, cs], kn_ref[0, 0, :, cs]], axis=0)
        vwin = jnp.concatenate([vp_ref[0, 0, :, cs], vc_ref[0, 0, :, cs], vn_ref[0, 0, :, cs]], axis=0)
        s = lax.dot_general(q, kwin, (((1,), (1,)), ((), ())), preferred_element_type=F32)
        s = s * scale + (bias_ref[hh] + edge)
        m = jnp.max(s, axis=-1, keepdims=True)
        p = jnp.exp(s - m)
        l = jnp.sum(p, axis=-1, keepdims=True)
        o = jnp.dot(p.astype(BF16), vwin, preferred_element_type=F32)
        o_ref[0, 0, :, cs] = o / l
        lz_all = jnp.where(lane_head == hh, m + jnp.log(l), lz_all)
    lz_ref[0, 0] = lz_all


def _attn_b_group(qkv, rel_bias, g, dil):
    batch, _, ln, _ = qkv.shape
    tq = min(B_TQ, ln)
    assert ln % tq == 0 and tq % B_HALO == 0
    n_tiles = ln // tq
    hpt = tq // B_HALO
    halo_blocks = ln // B_HALO
    w = tq + 2 * B_HALO
    bias, edges = _b_bias_tables(rel_bias, g, dil, tq)

    def cur(which):
        return pl.BlockSpec((1, 1, tq, B_GROUP_WIDTH), lambda b, c, i: (b, c, i, which))

    def prev(which):
        return pl.BlockSpec((1, 1, B_HALO, B_GROUP_WIDTH),
                            lambda b, c, i: (b, c, jnp.maximum(i * hpt - 1, 0), which))

    def nxt(which):
        return pl.BlockSpec((1, 1, B_HALO, B_GROUP_WIDTH),
                            lambda b, c, i: (b, c, jnp.minimum((i + 1) * hpt, halo_blocks - 1), which))

    def edge_row(i):
        return (i == 0).astype(jnp.int32) + 2 * (i == n_tiles - 1).astype(jnp.int32)

    return pl.pallas_call(
        _attn_b_kernel,
        grid=(batch, dil, n_tiles),
        in_specs=[
            cur(0), prev(1), cur(1), nxt(1), prev(2), cur(2), nxt(2),
            pl.BlockSpec((B_HEADS_PER_GROUP, tq, w), lambda b, c, i: (0, 0, 0)),
            pl.BlockSpec((1, 1, w), lambda b, c, i: (edge_row(i), 0, 0)),
        ],
        out_specs=[
            pl.BlockSpec((1, 1, tq, B_GROUP_WIDTH), lambda b, c, i: (b, c, i, 0)),
            pl.BlockSpec((1, 1, tq, LANES), lambda b, c, i: (b, c, i, 0)),
        ],
        out_shape=[
            jax.ShapeDtypeStruct((batch, dil, ln, B_GROUP_WIDTH), F32),
            jax.ShapeDtypeStruct((batch, dil, ln, LANES), F32),
        ],
        compiler_params=_cparams("parallel", "parallel", "parallel"),
        name=f"attn_b_g{g}",
    )(qkv, qkv, qkv, qkv, qkv, qkv, qkv, bias, edges)


def _mix_proj_kernel(o0_ref, o1_ref, o2_ref, z0_ref, z1_ref, z2_ref, w_ref, r_ref, out_ref,
                     y_ref, ot_ref, zt_ref):
    tm = out_ref.shape[0]

    def to_token_order(src_ref, dst_ref, dil):
        rows = tm // dil
        for c in range(dil):
            for k in range(dst_ref.shape[0]):
                dst_ref[k, pl.ds(c, rows, stride=dil), :] = src_ref[0, c, :, k * LANES:(k + 1) * LANES]

    z = []
    for g, z_ref in enumerate((z0_ref, z1_ref, z2_ref)):
        dil = B_GROUPS[g][1]
        if dil == 1:
            z.append(z_ref[0, 0])
        else:
            to_token_order(z_ref, zt_ref.at[g:g + 1], dil)
            z.append(zt_ref[g])
    zmax = functools.reduce(jnp.maximum, z)
    e = [jnp.exp(zz - zmax) for zz in z]
    den = functools.reduce(lambda a, b: a + b, e)
    lanes_per_head = LANES // B_HEADS_PER_GROUP
    for g, o_ref in enumerate((o0_ref, o1_ref, o2_ref)):
        dil = B_GROUPS[g][1]
        wt = e[g] / den
        if dil > 1:
            to_token_order(o_ref, ot_ref, dil)
        for hh in range(B_HEADS_PER_GROUP):
            wcol = wt[:, hh * lanes_per_head:hh * lanes_per_head + 1]
            cs = slice(hh * HEAD_DIM, (hh + 1) * HEAD_DIM)
            og = o_ref[0, 0, :, cs] if dil == 1 else ot_ref[hh]
            y_ref[:, g * B_GROUP_WIDTH + hh * HEAD_DIM:g * B_GROUP_WIDTH + (hh + 1) * HEAD_DIM] = (
                wcol * og).astype(BF16)

    out_ref[...] = r_ref[...] + jnp.dot(y_ref[...], w_ref[...], preferred_element_type=F32)


def _mix_proj(outs, lzs, w_stack, layer, resid, seq, tm):
    m = resid.shape[0]
    _, k, n = w_stack.shape
    tm = min(tm, seq)
    assert seq % tm == 0
    tiles_per_seq = seq // tm

    def sub_major(width, dil):
        assert tm % (dil * 8) == 0
        return pl.BlockSpec((1, dil, tm // dil, width),
                            lambda i: (i // tiles_per_seq, 0, i % tiles_per_seq, 0))

    dils = [dil for _, dil in B_GROUPS]
    return pl.pallas_call(
        _mix_proj_kernel,
        grid=(m // tm,),
        in_specs=[sub_major(B_GROUP_WIDTH, dil) for dil in dils] + [sub_major(LANES, dil) for dil in dils] + [
            pl.BlockSpec((None, k, n), lambda i: (layer, 0, 0), pipeline_mode=pl.Buffered(1)),
            pl.BlockSpec((tm, n), lambda i: (i, 0))],
        out_specs=pl.BlockSpec((tm, n), lambda i: (i, 0)),
        out_shape=jax.ShapeDtypeStruct((m, n), F32),
        scratch_shapes=[pltpu.VMEM((tm, k), BF16),
                        pltpu.VMEM((B_HEADS_PER_GROUP, tm, HEAD_DIM), F32),
                        pltpu.VMEM((len(dils), tm, LANES), F32)],
        compiler_params=_cparams("parallel"),
        name="mix_proj_b",
    )(*outs, *lzs, w_stack, resid)


def _ffn_up_kernel(xp_ref, x_ref, xn_ref, g_ref, wg_ref, wv_ref, cwg_ref, cwv_ref, cbg_ref, cbv_ref,
                   o_ref, xs_ref, *, seq):
    i = pl.program_id(0)
    tm = x_ref.shape[0]
    hl = CONV_HALO
    ext = tm + 2 * hl

    @pl.when(pl.program_id(1) == 0)
    def _():
        gain = g_ref[...]
        at_start = (i * tm) % seq == 0
        at_end = ((i + 1) * tm) % seq == 0
        xs_ref[0:hl, :] = jnp.where(at_start, 0.0, _rms(xp_ref[...], gain)).astype(BF16)
        xs_ref[hl:hl + tm, :] = _rms(x_ref[...], gain).astype(BF16)
        xs_ref[hl + tm:ext, :] = jnp.where(at_end, 0.0, _rms(xn_ref[...], gain)).astype(BF16)

    xs = xs_ref[...]

    def conv(w_ref, cw_ref, cb_ref):
        u = jnp.dot(xs, w_ref[...], preferred_element_type=F32)
        below = pltpu.roll(u, 1, 0)[hl:hl + tm]
        above = pltpu.roll(u, ext - 1, 0)[hl:hl + tm]
        cw = cw_ref[...]
        return below * cw[0:1] + u[hl:hl + tm] * cw[1:2] + above * cw[2:3] + cb_ref[...]

    gate = conv(wg_ref, cwg_ref, cbg_ref)
    val = conv(wv_ref, cwv_ref, cbv_ref)
    o_ref[...] = (gate * jax.nn.sigmoid(gate) * val).astype(BF16)


def _ffn_up(h, gain, w_up_stack, layer, conv_w, conv_b, seq, tm, tn):
    m, d = h.shape
    dff = w_up_stack.shape[2] // 2
    tm, tn = min(tm, seq), min(tn, dff)
    hl = CONV_HALO
    assert m % tm == 0 and seq % tm == 0 and dff % tn == 0 and tm % hl == 0
    nj = dff // tn
    rpt = tm // hl
    last_halo = m // hl - 1
    return pl.pallas_call(
        functools.partial(_ffn_up_kernel, seq=seq),
        grid=(m // tm, nj),
        in_specs=[
            pl.BlockSpec((hl, d), lambda i, j: (jnp.maximum(i * rpt - 1, 0), 0)),
            pl.BlockSpec((tm, d), lambda i, j: (i, 0)),
            pl.BlockSpec((hl, d), lambda i, j: (jnp.minimum((i + 1) * rpt, last_halo), 0)),
            pl.BlockSpec((1, d), lambda i, j: (0, 0)),
            pl.BlockSpec((None, d, tn), lambda i, j: (layer, 0, j)),
            pl.BlockSpec((None, d, tn), lambda i, j: (layer, 0, nj + j)),
            pl.BlockSpec((3, tn), lambda i, j: (0, j)),
            pl.BlockSpec((3, tn), lambda i, j: (0, nj + j)),
            pl.BlockSpec((1, tn), lambda i, j: (0, j)),
            pl.BlockSpec((1, tn), lambda i, j: (0, nj + j)),
        ],
        out_specs=pl.BlockSpec((tm, tn), lambda i, j: (i, j)),
        out_shape=jax.ShapeDtypeStruct((m, dff), BF16),
        scratch_shapes=[pltpu.VMEM((tm + 2 * hl, d), BF16)],
        compiler_params=_cparams("parallel", "arbitrary"),
        name="ffn_up",
    )(h, h, h, gain, w_up_stack, w_up_stack, conv_w, conv_w, conv_b, conv_b)


def _final_norm_kernel(x_ref, g_ref, o_ref):
    o_ref[...] = _rms(x_ref[...], g_ref[...])


def _final_norm(h, gain, tm):
    m, d = h.shape
    tm = min(tm, m)
    assert m % tm == 0
    return pl.pallas_call(
        _final_norm_kernel,
        grid=(m // tm,),
        in_specs=[pl.BlockSpec((tm, d), lambda i: (i, 0)), pl.BlockSpec((1, d), lambda i: (0, 0))],
        out_specs=pl.BlockSpec((tm, d), lambda i: (i, 0)),
        out_shape=jax.ShapeDtypeStruct((m, d), F32),
        compiler_params=_cparams("parallel"),
        name="final_norm",
    )(h, gain)


def _rope_tables(seq):
    rows = seq // GRID_W
    row_ids = jnp.repeat(jnp.arange(rows, dtype=F32), GRID_W)
    col_ids = jnp.tile(jnp.arange(GRID_W, dtype=F32), rows)
    quarter = HEAD_DIM // 4
    inv_freq = ROPE_THETA ** (-jnp.arange(quarter, dtype=F32) / quarter)
    ang_r = row_ids[:, None] * inv_freq[None, :]
    ang_c = col_ids[:, None] * inv_freq[None, :]
    cos_t = jnp.concatenate([jnp.cos(ang_r), jnp.cos(ang_r), jnp.cos(ang_c), jnp.cos(ang_c)], axis=-1)
    sin_t = jnp.concatenate([-jnp.sin(ang_r), jnp.sin(ang_r), -jnp.sin(ang_c), jnp.sin(ang_c)], axis=-1)
    return cos_t, sin_t


def kernel(x, a_w_qkv, a_w_o, a_q_gain, a_k_gain, b_w_qkv, b_w_o, rel_bias, mix_norm, ffn_norm, w_up, conv_w,
           conv_b, w_down, final_norm):
    batch, seq, d = x.shape
    m = batch * seq
    depth = mix_norm.shape[0]
    h = x.reshape(m, d)
    cos_t, sin_t = _rope_tables(seq)
    a_w_qkv, a_w_o, b_w_qkv, b_w_o, w_up, w_down = (
        w.astype(BF16) for w in (a_w_qkv, a_w_o, b_w_qkv, b_w_o, w_up, w_down))
    for i in range(depth):
        jj = i // 2
        gain = mix_norm[i][None, :]
        if i % 2 == 0:
            qkv = _qkv_a(h, gain, a_w_qkv, jj, cos_t, sin_t, a_q_gain[jj][None, :], a_k_gain[jj][None, :], seq)
            o = _attn_a(qkv, batch, seq)
            h = _matmul_resid(o, a_w_o, jj, h, TM // 2, "out_proj_a")
        else:
            outs, lzs = [], []
            for g, (_, dil) in enumerate(B_GROUPS):
                qkv = _qkv_b_group(h, gain, b_w_qkv, jj, g, dil, batch, seq, TM // 2, 2 * TN)
                o, lz = _attn_b_group(qkv, rel_bias, g, dil)
                outs.append(o)
                lzs.append(lz)
            h = _mix_proj(outs, lzs, b_w_o, jj, h, seq, TM // 4)
        act = _ffn_up(h, ffn_norm[i][None, :], w_up, i, conv_w[i], conv_b[i][None, :], seq, TM, TN)
        h = _matmul_resid(act, w_down, i, h, TM // 4, "ffn_down")
    return _final_norm(h, final_norm[None, :], TM // 2).reshape(batch, seq, d)
```

```python
import functools
import math

import jax
import jax.numpy as jnp
from jax import lax
from jax.experimental import pallas as pl
from jax.experimental.pallas import tpu as pltpu

GRID_W = 64
HEAD_DIM = 128
A_HEADS = 16
A_KV_HEADS = 4
A_GROUP = A_HEADS // A_KV_HEADS
ROPE_THETA = 10000.0
B_GROUPS = ((128, 1), (512, 4), (2048, 16))
B_HEADS_PER_GROUP = 8
B_GROUP_WIDTH = B_HEADS_PER_GROUP * HEAD_DIM
REL_BUCKETS = 32
REL_MAX_DISTANCE = 1024
EPS = 1e-6
NEG_INF = -1e30
LOG2E = 1.4426950408889634

V7X_VMEM_LIMIT_BYTES = 56 * 1024 * 1024
BF16_SUBLANES = 16
LANES = 128

TM = 1024
TN = 512
A_TQ = 256
A_TK = 512
A_UNROLL = 2
B_TQ = 128
B_HALO = 64
CONV_HALO = BF16_SUBLANES

F32 = jnp.float32
BF16 = jnp.bfloat16


def _cparams(*sem):
    return pltpu.CompilerParams(dimension_semantics=sem, vmem_limit_bytes=V7X_VMEM_LIMIT_BYTES)


def _rms(x, gain):
    ms = jnp.mean(x * x, axis=-1, keepdims=True)
    return x * lax.rsqrt(ms + EPS) * gain


def _qkv_a_kernel(x_ref, g_ref, w_ref, cos_ref, sin_ref, qg_ref, kg_ref, o_ref, xn_ref, *, n_qk_tiles, n_q_tiles):
    j = pl.program_id(1)

    @pl.when(j == 0)
    def _():
        xn_ref[...] = _rms(x_ref[...], g_ref[...]).astype(BF16)

    y = jnp.dot(xn_ref[...], w_ref[...], preferred_element_type=F32)

    @pl.when(j < n_qk_tiles)
    def _():
        gain = jnp.where(j < n_q_tiles, qg_ref[...] * (HEAD_DIM ** -0.5 * LOG2E), kg_ref[...])
        cos = cos_ref[...]
        sin = sin_ref[...]
        lane = lax.broadcasted_iota(jnp.int32, (1, HEAD_DIM), 1)
        first_quarter = (lane % (HEAD_DIM // 2)) < (HEAD_DIM // 4)
        for hh in range(y.shape[1] // HEAD_DIM):
            yh = y[:, hh * HEAD_DIM:(hh + 1) * HEAD_DIM]
            yn = _rms(yh, gain)
            partner = jnp.where(first_quarter,
                                pltpu.roll(yn, HEAD_DIM - HEAD_DIM // 4, 1),
                                pltpu.roll(yn, HEAD_DIM // 4, 1))
            o_ref[:, hh * HEAD_DIM:(hh + 1) * HEAD_DIM] = (yn * cos + partner * sin).astype(BF16)

    @pl.when(j >= n_qk_tiles)
    def _():
        o_ref[...] = y.astype(BF16)


def _qkv_a(h, gain, w_stack, layer, cos_t, sin_t, q_gain, k_gain, seq):
    m, d = h.shape
    n = w_stack.shape[2]
    tm, tn = min(TM, seq), TN
    nq = A_HEADS * HEAD_DIM
    nk = A_KV_HEADS * HEAD_DIM
    assert m % tm == 0 and seq % tm == 0 and n % tn == 0 and nq % tn == 0 and (nq + nk) % tn == 0
    pos_blocks = seq // tm
    kern = functools.partial(_qkv_a_kernel, n_qk_tiles=(nq + nk) // tn, n_q_tiles=nq // tn)
    return pl.pallas_call(
        kern,
        grid=(m // tm, n // tn),
        in_specs=[
            pl.BlockSpec((tm, d), lambda i, j: (i, 0)),
            pl.BlockSpec((1, d), lambda i, j: (0, 0)),
            pl.BlockSpec((None, d, tn), lambda i, j: (layer, 0, j)),
            pl.BlockSpec((tm, HEAD_DIM), lambda i, j: (i % pos_blocks, 0)),
            pl.BlockSpec((tm, HEAD_DIM), lambda i, j: (i % pos_blocks, 0)),
            pl.BlockSpec((1, HEAD_DIM), lambda i, j: (0, 0)),
            pl.BlockSpec((1, HEAD_DIM), lambda i, j: (0, 0)),
        ],
        out_specs=pl.BlockSpec((tm, tn), lambda i, j: (i, j)),
        out_shape=jax.ShapeDtypeStruct((m, n), BF16),
        scratch_shapes=[pltpu.VMEM((tm, d), BF16)],
        compiler_params=_cparams("parallel", "arbitrary"),
        name="qkv_a",
    )(h, gain, w_stack, cos_t, sin_t, q_gain, k_gain)


def _attn_a_kernel(qt_ref, k_ref, vt_ref, o_ref, acc_ref, s_ref, *, tk):
    tq = qt_ref.shape[1]
    seq = k_ref.shape[0]
    n_chunks = seq // tk
    acc_ref[...] = jnp.zeros_like(acc_ref)
    ones_rows = jnp.ones((BF16_SUBLANES, tk), BF16)

    def scores(t, g, slot):
        start = pl.multiple_of(t * tk, tk)
        s_ref[slot, g] = jnp.dot(k_ref[pl.ds(start, tk), :], qt_ref[g * HEAD_DIM:(g + 1) * HEAD_DIM, :],
                                 preferred_element_type=F32)

    def chunk(t, t_next, slot, ms):
        start = pl.multiple_of(t * tk, tk)
        vt = jnp.concatenate([vt_ref[:, pl.ds(start, tk)], ones_rows], axis=0)
        new_ms = []
        for g in range(A_GROUP):
            scores(t_next, g, 1 - slot)
            s = s_ref[slot, g]
            m_new = jnp.maximum(ms[g], jnp.max(s, axis=0, keepdims=True))
            alpha = jnp.exp2(ms[g] - m_new)
            p = jnp.exp2(s - m_new)
            new_ms.append(m_new)
            cols = slice(g * tq, (g + 1) * tq)
            acc_ref[:, cols] = alpha * acc_ref[:, cols] + jnp.dot(vt, p.astype(BF16), preferred_element_type=F32)
        return tuple(new_ms)

    def body(i, ms):
        for u in range(A_UNROLL):
            t = A_UNROLL * i + u
            ms = chunk(t, jnp.minimum(t + 1, n_chunks - 1), u % 2, ms)
        return ms

    for g in range(A_GROUP):
        scores(0, g, 0)
    m0 = tuple(jnp.full((1, tq), NEG_INF, F32) for _ in range(A_GROUP))
    lax.fori_loop(0, n_chunks // A_UNROLL, body, m0)
    for g in range(A_GROUP):
        cols = slice(g * tq, (g + 1) * tq)
        out_t = acc_ref[0:HEAD_DIM, cols] / acc_ref[HEAD_DIM:HEAD_DIM + 1, cols]
        o_ref[:, g * HEAD_DIM:(g + 1) * HEAD_DIM] = out_t.T.astype(BF16)


def _attn_a(qkv, batch, seq):
    m = qkv.shape[0]
    tq = min(A_TQ, seq)
    tk = min(A_TK, seq)
    assert seq % tq == 0 and seq % (A_UNROLL * tk) == 0 and A_UNROLL % 2 == 0
    nq = A_HEADS * HEAD_DIM
    nk = A_KV_HEADS * HEAD_DIM
    gw = A_GROUP * HEAD_DIM
    q_blocks = seq // tq
    q_t = qkv[:, :nq].T
    v_t = qkv[:, nq + nk:].T
    return pl.pallas_call(
        functools.partial(_attn_a_kernel, tk=tk),
        grid=(batch, A_KV_HEADS, q_blocks),
        in_specs=[
            pl.BlockSpec((gw, tq), lambda b, kv, qi: (kv, b * q_blocks + qi)),
            pl.BlockSpec((seq, HEAD_DIM), lambda b, kv, qi: (b, A_HEADS + kv)),
            pl.BlockSpec((HEAD_DIM, seq), lambda b, kv, qi: (kv, b)),
        ],
        out_specs=pl.BlockSpec((tq, gw), lambda b, kv, qi: (b * q_blocks + qi, kv)),
        out_shape=jax.ShapeDtypeStruct((m, nq), BF16),
        scratch_shapes=[pltpu.VMEM((HEAD_DIM + BF16_SUBLANES, A_GROUP * tq), F32),
                        pltpu.VMEM((2, A_GROUP, tk, tq), F32)],
        compiler_params=_cparams("parallel", "parallel", "arbitrary"),
        name="attn_a",
    )(q_t, qkv, v_t)


def _matmul_resid_kernel(x_ref, w_ref, r_ref, o_ref):
    o_ref[...] = r_ref[...] + jnp.dot(x_ref[...], w_ref[...], preferred_element_type=F32)


def _matmul_resid(x, w_stack, layer, resid, tm, name):
    m, k = x.shape
    n = w_stack.shape[2]
    tm = min(tm, m)
    assert m % tm == 0
    return pl.pallas_call(
        _matmul_resid_kernel,
        grid=(m // tm,),
        in_specs=[
            pl.BlockSpec((tm, k), lambda i: (i, 0)),
            pl.BlockSpec((None, k, n), lambda i: (layer, 0, 0), pipeline_mode=pl.Buffered(1)),
            pl.BlockSpec((tm, n), lambda i: (i, 0)),
        ],
        out_specs=pl.BlockSpec((tm, n), lambda i: (i, 0)),
        out_shape=jax.ShapeDtypeStruct((m, n), F32),
        compiler_params=_cparams("parallel"),
        name=name,
    )(x, w_stack, resid)


def _qkv_b_kernel(x_ref, xnext_ref, g_ref, w_ref, o_ref, xa_ref, xb_ref, *lane_chunks, dil):
    tm, d = xnext_ref.shape
    n = tm // dil
    gain = g_ref[...]

    def prepare(src_ref, row0, dst_ref):
        if dil == 1:
            dst_ref[...] = _rms(src_ref[row0:row0 + tm, :], gain).astype(BF16)
            return
        xc_ref, = lane_chunks
        for k in range(d // LANES):
            xc_ref[k] = src_ref[row0:row0 + tm, k * LANES:(k + 1) * LANES]
        for c in range(dil):
            rows = jnp.concatenate(
                [xc_ref[k, pl.ds(c, n, stride=dil), :] for k in range(d // LANES)], axis=1)
            dst_ref[c * n:(c + 1) * n, :] = _rms(rows, gain).astype(BF16)

    def project(src_ref, half):
        y = jnp.dot(src_ref[...], w_ref[...], preferred_element_type=F32)
        for c in range(dil):
            o_ref[0, c, half * n:(half + 1) * n, :] = y[c * n:(c + 1) * n].astype(BF16)

    @pl.when(pl.program_id(0) == 0)
    def _():
        prepare(x_ref, 0, xa_ref)

    project(xa_ref, 0)
    prepare(x_ref, tm, xb_ref)
    project(xb_ref, 1)
    prepare(xnext_ref, 0, xa_ref)


def _qkv_b_group(h, gain, w_stack, layer, g, dil, batch, seq, tm):
    m, d = h.shape
    gw = 3 * B_GROUP_WIDTH
    tm = min(tm, seq // 2)
    assert seq % (2 * tm) == 0 and tm % (dil * BF16_SUBLANES) == 0
    steps_per_seq = seq // (2 * tm)
    n = tm // dil
    last_tile = m // tm - 1
    return pl.pallas_call(
        functools.partial(_qkv_b_kernel, dil=dil),
        grid=(m // (2 * tm),),
        in_specs=[
            pl.BlockSpec((2 * tm, d), lambda i: (i, 0)),
            pl.BlockSpec((tm, d), lambda i: (jnp.minimum(2 * i + 2, last_tile), 0)),
            pl.BlockSpec((1, d), lambda i: (0, 0)),
            pl.BlockSpec((None, d, gw), lambda i: (layer, 0, g), pipeline_mode=pl.Buffered(1)),
        ],
        out_specs=pl.BlockSpec((1, dil, 2 * n, gw), lambda i: (i // steps_per_seq, 0, i % steps_per_seq, 0)),
        out_shape=jax.ShapeDtypeStruct((batch, dil, seq // dil, gw), BF16),
        scratch_shapes=[pltpu.VMEM((tm, d), BF16), pltpu.VMEM((tm, d), BF16)] + (
            [pltpu.VMEM((d // LANES, tm, LANES), F32)] if dil > 1 else []),
        compiler_params=_cparams("arbitrary"),
        name=f"qkv_b_g{g}",
    )(h, h, gain, w_stack)


def _t5_bucket(rel):
    nb = REL_BUCKETS // 2
    max_exact = nb // 2
    base = jnp.where(rel > 0, nb, 0)
    n = jnp.abs(rel)
    nf = jnp.maximum(n, 1).astype(F32)
    large = max_exact + (jnp.log(nf / max_exact) / math.log(REL_MAX_DISTANCE / max_exact)
                         * (nb - max_exact)).astype(jnp.int32)
    large = jnp.minimum(large, nb - 1)
    return base + jnp.where(n < max_exact, n, large)


def _b_bias_tables(rel_bias, g, dil, tq):
    w = tq + 2 * B_HALO
    kk = jnp.arange(w)
    rel = kk[None, :] - B_HALO - jnp.arange(tq)[:, None]
    hg = B_HEADS_PER_GROUP
    bias = rel_bias[:, g * hg:(g + 1) * hg][_t5_bucket(rel * dil)].astype(F32).transpose(2, 0, 1)
    bias = jnp.where((jnp.abs(rel) <= B_HALO)[None], bias, NEG_INF)
    edges = []
    for first, last in ((False, False), (True, False), (False, True), (True, True)):
        off = jnp.zeros((w,), bool)
        if first:
            off = off | (kk < B_HALO)
        if last:
            off = off | (kk >= tq + B_HALO)
        edges.append(jnp.where(off, NEG_INF, 0.0).astype(F32)[None, :])
    return bias, jnp.stack(edges)


def _attn_b_kernel(q_ref, kp_ref, kc_ref, kn_ref, vp_ref, vc_ref, vn_ref, bias_ref, edge_ref, o_ref, lz_ref):
    tq = q_ref.shape[2]
    scale = HEAD_DIM ** -0.5
    lane_head = lax.broadcasted_iota(jnp.int32, (1, LANES), 1) // (LANES // B_HEADS_PER_GROUP)
    lz_all = jnp.zeros((tq, LANES), F32)
    edge = edge_ref[0]
    for hh in range(B_HEADS_PER_GROUP):
        cs = slice(hh * HEAD_DIM, (hh + 1) * HEAD_DIM)
        q = q_ref[0, 0, :, cs]
        kwin = jnp.concatenate([kp_ref[0, 0, :, cs], kc_ref[0, 0, :, cs], kn_ref[0, 0, :, cs]], axis=0)
        vwin = jnp.concatenate([vp_ref[0, 0, :, cs], vc_ref[0, 0, :, cs], vn_ref[0, 0, :, cs]], axis=0)
        s = lax.dot_general(q, kwin, (((1,), (1,)), ((), ())), preferred_element_type=F32)
        s = s * scale + (bias_ref[hh] + edge)
        m = jnp.max(s, axis=-1, keepdims=True)
        p = jnp.exp(s - m)
        l = jnp.sum(p, axis=-1, keepdims=True)
        o = jnp.dot(p.astype(BF16), vwin, preferred_element_type=F32)
        o_ref[0, 0, :, cs] = o / l
        lz_all = jnp.where(lane_head == hh, m + jnp.log(l), lz_all)
    lz_ref[0, 0] = lz_all


def _attn_b_group(qkv, rel_bias, g, dil):
    batch, _, ln, _ = qkv.shape
    tq = min(B_TQ, ln)
    assert ln % tq == 0 and tq % B_HALO == 0
    n_tiles = ln // tq
    hpt = tq // B_HALO
    halo_blocks = ln // B_HALO
    w = tq + 2 * B_HALO
    bias, edges = _b_bias_tables(rel_bias, g, dil, tq)

    def cur(which):
        return pl.BlockSpec((1, 1, tq, B_GROUP_WIDTH), lambda b, c, i: (b, c, i, which))

    def prev(which):
        return pl.BlockSpec((1, 1, B_HALO, B_GROUP_WIDTH),
                            lambda b, c, i: (b, c, jnp.maximum(i * hpt - 1, 0), which))

    def nxt(which):
        return pl.BlockSpec((1, 1, B_HALO, B_GROUP_WIDTH),
                            lambda b, c, i: (b, c, jnp.minimum((i + 1) * hpt, halo_blocks - 1), which))

    def edge_row(i):
        return (i == 0).astype(jnp.int32) + 2 * (i == n_tiles - 1).astype(jnp.int32)

    return pl.pallas_call(
        _attn_b_kernel,
        grid=(batch, dil, n_tiles),
        in_specs=[
            cur(0), prev(1), cur(1), nxt(1), prev(2), cur(2), nxt(2),
            pl.BlockSpec((B_HEADS_PER_GROUP, tq, w), lambda b, c, i: (0, 0, 0)),
            pl.BlockSpec((1, 1, w), lambda b, c, i: (edge_row(i), 0, 0)),
        ],
        out_specs=[
            pl.BlockSpec((1, 1, tq, B_GROUP_WIDTH), lambda b, c, i: (b, c, i, 0)),
            pl.BlockSpec((1, 1, tq, LANES), lambda b, c, i: (b, c, i, 0)),
        ],
        out_shape=[
            jax.ShapeDtypeStruct((batch, dil, ln, B_GROUP_WIDTH), F32),
            jax.ShapeDtypeStruct((batch, dil, ln, LANES), F32),
        ],
        compiler_params=_cparams("parallel", "parallel", "parallel"),
        name=f"attn_b_g{g}",
    )(qkv, qkv, qkv, qkv, qkv, qkv, qkv, bias, edges)


def _mix_proj_kernel(o0_ref, o1_ref, o2_ref, z0_ref, z1_ref, z2_ref, w_ref, r_ref, out_ref,
                     y_ref, ot_ref, zt_ref):
    tm = out_ref.shape[0]

    def to_token_order(src_ref, dst_ref, dil):
        rows = tm // dil
        for c in range(dil):
            for k in range(dst_ref.shape[0]):
                dst_ref[k, pl.ds(c, rows, stride=dil), :] = src_ref[0, c, :, k * LANES:(k + 1) * LANES]

    z = []
    for g, z_ref in enumerate((z0_ref, z1_ref, z2_ref)):
        dil = B_GROUPS[g][1]
        if dil == 1:
            z.append(z_ref[0, 0])
        else:
            to_token_order(z_ref, zt_ref.at[g:g + 1], dil)
            z.append(zt_ref[g])
    zmax = functools.reduce(jnp.maximum, z)
    e = [jnp.exp(zz - zmax) for zz in z]
    den = functools.reduce(lambda a, b: a + b, e)
    lanes_per_head = LANES // B_HEADS_PER_GROUP
    for g, o_ref in enumerate((o0_ref, o1_ref, o2_ref)):
        dil = B_GROUPS[g][1]
        wt = e[g] / den
        if dil > 1:
            to_token_order(o_ref, ot_ref, dil)
        for hh in range(B_HEADS_PER_GROUP):
            wcol = wt[:, hh * lanes_per_head:hh * lanes_per_head + 1]
            cs = slice(hh * HEAD_DIM, (hh + 1) * HEAD_DIM)
            og = o_ref[0, 0, :, cs] if dil == 1 else ot_ref[hh]
            y_ref[:, g * B_GROUP_WIDTH + hh * HEAD_DIM:g * B_GROUP_WIDTH + (hh + 1) * HEAD_DIM] = (
                wcol * og).astype(BF16)

    out_ref[...] = r_ref[...] + jnp.dot(y_ref[...], w_ref[...], preferred_element_type=F32)


def _mix_proj(outs, lzs, w_stack, layer, resid, seq, tm):
    m = resid.shape[0]
    _, k, n = w_stack.shape
    tm = min(tm, seq)
    assert seq % tm == 0
    tiles_per_seq = seq // tm

    def sub_major(width, dil):
        assert tm % (dil * 8) == 0
        return pl.BlockSpec((1, dil, tm // dil, width),
                            lambda i: (i // tiles_per_seq, 0, i % tiles_per_seq, 0))

    dils = [dil for _, dil in B_GROUPS]
    return pl.pallas_call(
        _mix_proj_kernel,
        grid=(m // tm,),
        in_specs=[sub_major(B_GROUP_WIDTH, dil) for dil in dils] + [sub_major(LANES, dil) for dil in dils] + [
            pl.BlockSpec((None, k, n), lambda i: (layer, 0, 0), pipeline_mode=pl.Buffered(1)),
            pl.BlockSpec((tm, n), lambda i: (i, 0))],
        out_specs=pl.BlockSpec((tm, n), lambda i: (i, 0)),
        out_shape=jax.ShapeDtypeStruct((m, n), F32),
        scratch_shapes=[pltpu.VMEM((tm, k), BF16),
                        pltpu.VMEM((B_HEADS_PER_GROUP, tm, HEAD_DIM), F32),
                        pltpu.VMEM((len(dils), tm, LANES), F32)],
        compiler_params=_cparams("parallel"),
        name="mix_proj_b",
    )(*outs, *lzs, w_stack, resid)


def _ffn_up_kernel(xp_ref, x_ref, xn_ref, g_ref, wg_ref, wv_ref, cwg_ref, cwv_ref, cbg_ref, cbv_ref,
                   o_ref, xs_ref, *, seq):
    i = pl.program_id(0)
    tm = x_ref.shape[0]
    hl = CONV_HALO
    ext = tm + 2 * hl

    @pl.when(pl.program_id(1) == 0)
    def _():
        gain = g_ref[...]
        at_start = (i * tm) % seq == 0
        at_end = ((i + 1) * tm) % seq == 0
        xs_ref[0:hl, :] = jnp.where(at_start, 0.0, _rms(xp_ref[...], gain)).astype(BF16)
        xs_ref[hl:hl + tm, :] = _rms(x_ref[...], gain).astype(BF16)
        xs_ref[hl + tm:ext, :] = jnp.where(at_end, 0.0, _rms(xn_ref[...], gain)).astype(BF16)

    xs = xs_ref[...]

    def conv(w_ref, cw_ref, cb_ref):
        u = jnp.dot(xs, w_ref[...], preferred_element_type=F32)
        below = pltpu.roll(u, 1, 0)[hl:hl + tm]
        above = pltpu.roll(u, ext - 1, 0)[hl:hl + tm]
        cw = cw_ref[...]
        return below * cw[0:1] + u[hl:hl + tm] * cw[1:2] + above * cw[2:3] + cb_ref[...]

    gate = conv(wg_ref, cwg_ref, cbg_ref)
    val = conv(wv_ref, cwv_ref, cbv_ref)
    o_ref[...] = (gate * jax.nn.sigmoid(gate) * val).astype(BF16)


def _ffn_up(h, gain, w_up_stack, layer, conv_w, conv_b, seq, tm, tn):
    m, d = h.shape
    dff = w_up_stack.shape[2] // 2
    tm, tn = min(tm, seq), min(tn, dff)
    hl = CONV_HALO
    assert m % tm == 0 and seq % tm == 0 and dff % tn == 0 and tm % hl == 0
    nj = dff // tn
    rpt = tm // hl
    last_halo = m // hl - 1
    return pl.pallas_call(
        functools.partial(_ffn_up_kernel, seq=seq),
        grid=(m // tm, nj),
        in_specs=[
            pl.BlockSpec((hl, d), lambda i, j: (jnp.maximum(i * rpt - 1, 0), 0)),
            pl.BlockSpec((tm, d), lambda i, j: (i, 0)),
            pl.BlockSpec((hl, d), lambda i, j: (jnp.minimum((i + 1) * rpt, last_halo), 0)),
            pl.BlockSpec((1, d), lambda i, j: (0, 0)),
            pl.BlockSpec((None, d, tn), lambda i, j: (layer, 0, j)),
            pl.BlockSpec((None, d, tn), lambda i, j: (layer, 0, nj + j)),
            pl.BlockSpec((3, tn), lambda i, j: (0, j)),
            pl.BlockSpec((3, tn), lambda i, j: (0, nj + j)),
            pl.BlockSpec((1, tn), lambda i, j: (0, j)),
            pl.BlockSpec((1, tn), lambda i, j: (0, nj + j)),
        ],
        out_specs=pl.BlockSpec((tm, tn), lambda i, j: (i, j)),
        out_shape=jax.ShapeDtypeStruct((m, dff), BF16),
        scratch_shapes=[pltpu.VMEM((tm + 2 * hl, d), BF16)],
        compiler_params=_cparams("parallel", "arbitrary"),
        name="ffn_up",
    )(h, h, h, gain, w_up_stack, w_up_stack, conv_w, conv_w, conv_b, conv_b)


def _final_norm_kernel(x_ref, g_ref, o_ref):
    o_ref[...] = _rms(x_ref[...], g_ref[...])


def _final_norm(h, gain, tm):
    m, d = h.shape
    tm = min(tm, m)
    assert m % tm == 0
    return pl.pallas_call(
        _final_norm_kernel,
        grid=(m // tm,),
        in_specs=[pl.BlockSpec((tm, d), lambda i: (i, 0)), pl.BlockSpec((1, d), lambda i: (0, 0))],
        out_specs=pl.BlockSpec((tm, d), lambda i: (i, 0)),
        out_shape=jax.ShapeDtypeStruct((m, d), F32),
        compiler_params=_cparams("parallel"),
        name="final_norm",
    )(h, gain)


def _rope_tables(seq):
    rows = seq // GRID_W
    row_ids = jnp.repeat(jnp.arange(rows, dtype=F32), GRID_W)
    col_ids = jnp.tile(jnp.arange(GRID_W, dtype=F32), rows)
    quarter = HEAD_DIM // 4
    inv_freq = ROPE_THETA ** (-jnp.arange(quarter, dtype=F32) / quarter)
    ang_r = row_ids[:, None] * inv_freq[None, :]
    ang_c = col_ids[:, None] * inv_freq[None, :]
    cos_t = jnp.concatenate([jnp.cos(ang_r), jnp.cos(ang_r), jnp.cos(ang_c), jnp.cos(ang_c)], axis=-1)
    sin_t = jnp.concatenate([-jnp.sin(ang_r), jnp.sin(ang_r), -jnp.sin(ang_c), jnp.sin(ang_c)], axis=-1)
    return cos_t, sin_t


def kernel(x, a_w_qkv, a_w_o, a_q_gain, a_k_gain, b_w_qkv, b_w_o, rel_bias, mix_norm, ffn_norm, w_up, conv_w,
           conv_b, w_down, final_norm):
    batch, seq, d = x.shape
    m = batch * seq
    depth = mix_norm.shape[0]
    h = x.reshape(m, d)
    cos_t, sin_t = _rope_tables(seq)
    a_w_qkv, a_w_o, b_w_qkv, b_w_o, w_up, w_down = (
        w.astype(BF16) for w in (a_w_qkv, a_w_o, b_w_qkv, b_w_o, w_up, w_down))
    for i in range(depth):
        jj = i // 2
        gain = mix_norm[i][None, :]
        if i % 2 == 0:
            qkv = _qkv_a(h, gain, a_w_qkv, jj, cos_t, sin_t, a_q_gain[jj][None, :], a_k_gain[jj][None, :], seq)
            o = _attn_a(qkv, batch, seq)
            h = _matmul_resid(o, a_w_o, jj, h, TM // 2, "out_proj_a")
        else:
            outs, lzs = [], []
            for g, (_, dil) in enumerate(B_GROUPS):
                qkv = _qkv_b_group(h, gain, b_w_qkv, jj, g, dil, batch, seq, TM // 4)
                o, lz = _attn_b_group(qkv, rel_bias, g, dil)
                outs.append(o)
                lzs.append(lz)
            h = _mix_proj(outs, lzs, b_w_o, jj, h, seq, TM // 4)
        act = _ffn_up(h, ffn_norm[i][None, :], w_up, i, conv_w[i], conv_b[i][None, :], seq, TM, TN)
        h = _matmul_resid(act, w_down, i, h, TM // 4, "ffn_down")
    return _final_norm(h, final_norm[None, :], TM // 2).reshape(batch, seq, d)
```

```python
import functools
import math

import jax
import jax.numpy as jnp
from jax import lax
from jax.experimental import pallas as pl
from jax.experimental.pallas import tpu as pltpu

GRID_W = 64
HEAD_DIM = 128
A_HEADS = 16
A_KV_HEADS = 4
A_GROUP = A_HEADS // A_KV_HEADS
ROPE_THETA = 10000.0
B_GROUPS = ((128, 1), (512, 4), (2048, 16))
B_HEADS_PER_GROUP = 8
B_GROUP_WIDTH = B_HEADS_PER_GROUP * HEAD_DIM
REL_BUCKETS = 32
REL_MAX_DISTANCE = 1024
EPS = 1e-6
NEG_INF = -1e30
LOG2E = 1.4426950408889634

V7X_VMEM_LIMIT_BYTES = 56 * 1024 * 1024
BF16_SUBLANES = 16
LANES = 128

TM = 1024
TN = 512
A_TQ = 256
A_TK = 512
A_UNROLL = 2
B_TQ = 128
B_HALO = 64
CONV_HALO = BF16_SUBLANES

F32 = jnp.float32
BF16 = jnp.bfloat16


def _cparams(*sem):
    return pltpu.CompilerParams(dimension_semantics=sem, vmem_limit_bytes=V7X_VMEM_LIMIT_BYTES)


def _rms(x, gain):
    ms = jnp.mean(x * x, axis=-1, keepdims=True)
    return x * lax.rsqrt(ms + EPS) * gain


def _qkv_a_kernel(x_ref, g_ref, w_ref, cos_ref, sin_ref, qg_ref, kg_ref, o_ref, xn_ref, *, n_qk_tiles, n_q_tiles):
    j = pl.program_id(1)

    @pl.when(j == 0)
    def _():
        xn_ref[...] = _rms(x_ref[...], g_ref[...]).astype(BF16)

    y = jnp.dot(xn_ref[...], w_ref[...], preferred_element_type=F32)

    @pl.when(j < n_qk_tiles)
    def _():
        gain = jnp.where(j < n_q_tiles, qg_ref[...] * (HEAD_DIM ** -0.5 * LOG2E), kg_ref[...])
        cos = cos_ref[...]
        sin = sin_ref[...]
        lane = lax.broadcasted_iota(jnp.int32, (1, HEAD_DIM), 1)
        first_quarter = (lane % (HEAD_DIM // 2)) < (HEAD_DIM // 4)
        for hh in range(y.shape[1] // HEAD_DIM):
            yh = y[:, hh * HEAD_DIM:(hh + 1) * HEAD_DIM]
            yn = _rms(yh, gain)
            partner = jnp.where(first_quarter,
                                pltpu.roll(yn, HEAD_DIM - HEAD_DIM // 4, 1),
                                pltpu.roll(yn, HEAD_DIM // 4, 1))
            o_ref[:, hh * HEAD_DIM:(hh + 1) * HEAD_DIM] = (yn * cos + partner * sin).astype(BF16)

    @pl.when(j >= n_qk_tiles)
    def _():
        o_ref[...] = y.astype(BF16)


def _qkv_a(h, gain, w_stack, layer, cos_t, sin_t, q_gain, k_gain, seq):
    m, d = h.shape
    n = w_stack.shape[2]
    tm, tn = min(TM, seq), TN
    nq = A_HEADS * HEAD_DIM
    nk = A_KV_HEADS * HEAD_DIM
    assert m % tm == 0 and seq % tm == 0 and n % tn == 0 and nq % tn == 0 and (nq + nk) % tn == 0
    pos_blocks = seq // tm
    kern = functools.partial(_qkv_a_kernel, n_qk_tiles=(nq + nk) // tn, n_q_tiles=nq // tn)
    return pl.pallas_call(
        kern,
        grid=(m // tm, n // tn),
        in_specs=[
            pl.BlockSpec((tm, d), lambda i, j: (i, 0)),
            pl.BlockSpec((1, d), lambda i, j: (0, 0)),
            pl.BlockSpec((None, d, tn), lambda i, j: (layer, 0, j)),
            pl.BlockSpec((tm, HEAD_DIM), lambda i, j: (i % pos_blocks, 0)),
            pl.BlockSpec((tm, HEAD_DIM), lambda i, j: (i % pos_blocks, 0)),
            pl.BlockSpec((1, HEAD_DIM), lambda i, j: (0, 0)),
            pl.BlockSpec((1, HEAD_DIM), lambda i, j: (0, 0)),
        ],
        out_specs=pl.BlockSpec((tm, tn), lambda i, j: (i, j)),
        out_shape=jax.ShapeDtypeStruct((m, n), BF16),
        scratch_shapes=[pltpu.VMEM((tm, d), BF16)],
        compiler_params=_cparams("parallel", "arbitrary"),
        name="qkv_a",
    )(h, gain, w_stack, cos_t, sin_t, q_gain, k_gain)


def _attn_a_kernel(qt_ref, k_ref, vt_ref, o_ref, acc_ref, s_ref, *, tk):
    tq = qt_ref.shape[1]
    seq = k_ref.shape[0]
    n_chunks = seq // tk
    acc_ref[...] = jnp.zeros_like(acc_ref)
    ones_rows = jnp.ones((BF16_SUBLANES, tk), BF16)

    def scores(t, g, slot):
        start = pl.multiple_of(t * tk, tk)
        s_ref[slot, g] = jnp.dot(k_ref[pl.ds(start, tk), :], qt_ref[g * HEAD_DIM:(g + 1) * HEAD_DIM, :],
                                 preferred_element_type=F32)

    def chunk(t, t_next, slot, ms):
        start = pl.multiple_of(t * tk, tk)
        vt = jnp.concatenate([vt_ref[:, pl.ds(start, tk)], ones_rows], axis=0)
        new_ms = []
        for g in range(A_GROUP):
            scores(t_next, g, 1 - slot)
            s = s_ref[slot, g]
            m_new = jnp.maximum(ms[g], jnp.max(s, axis=0, keepdims=True))
            alpha = jnp.exp2(ms[g] - m_new)
            p = jnp.exp2(s - m_new)
            new_ms.append(m_new)
            cols = slice(g * tq, (g + 1) * tq)
            acc_ref[:, cols] = alpha * acc_ref[:, cols] + jnp.dot(vt, p.astype(BF16), preferred_element_type=F32)
        return tuple(new_ms)

    def body(i, ms):
        for u in range(A_UNROLL):
            t = A_UNROLL * i + u
            ms = chunk(t, jnp.minimum(t + 1, n_chunks - 1), u % 2, ms)
        return ms

    for g in range(A_GROUP):
        scores(0, g, 0)
    m0 = tuple(jnp.full((1, tq), NEG_INF, F32) for _ in range(A_GROUP))
    lax.fori_loop(0, n_chunks // A_UNROLL, body, m0)
    for g in range(A_GROUP):
        cols = slice(g * tq, (g + 1) * tq)
        out_t = acc_ref[0:HEAD_DIM, cols] / acc_ref[HEAD_DIM:HEAD_DIM + 1, cols]
        o_ref[:, g * HEAD_DIM:(g + 1) * HEAD_DIM] = out_t.T.astype(BF16)


def _attn_a(qkv, batch, seq):
    m = qkv.shape[0]
    tq = min(A_TQ, seq)
    tk = min(A_TK, seq)
    assert seq % tq == 0 and seq % (A_UNROLL * tk) == 0 and A_UNROLL % 2 == 0
    nq = A_HEADS * HEAD_DIM
    nk = A_KV_HEADS * HEAD_DIM
    gw = A_GROUP * HEAD_DIM
    q_blocks = seq // tq
    q_t = qkv[:, :nq].T
    v_t = qkv[:, nq + nk:].T
    return pl.pallas_call(
        functools.partial(_attn_a_kernel, tk=tk),
        grid=(batch, A_KV_HEADS, q_blocks),
        in_specs=[
            pl.BlockSpec((gw, tq), lambda b, kv, qi: (kv, b * q_blocks + qi)),
            pl.BlockSpec((seq, HEAD_DIM), lambda b, kv, qi: (b, A_HEADS + kv)),
            pl.BlockSpec((HEAD_DIM, seq), lambda b, kv, qi: (kv, b)),
        ],
        out_specs=pl.BlockSpec((tq, gw), lambda b, kv, qi: (b * q_blocks + qi, kv)),
        out_shape=jax.ShapeDtypeStruct((m, nq), BF16),
        scratch_shapes=[pltpu.VMEM((HEAD_DIM + BF16_SUBLANES, A_GROUP * tq), F32),
                        pltpu.VMEM((2, A_GROUP, tk, tq), F32)],
        compiler_params=_cparams("parallel", "parallel", "arbitrary"),
        name="attn_a",
    )(q_t, qkv, v_t)


def _matmul_resid_kernel(x_ref, w_ref, r_ref, *rest):
    h = r_ref[...] + jnp.dot(x_ref[...], w_ref[...], preferred_element_type=F32)
    if len(rest) == 2:
        g_ref, o_ref = rest
        o_ref[...] = _rms(h, g_ref[...])
    else:
        rest[0][...] = h


def _matmul_resid(x, w_stack, layer, resid, tm, name, out_gain=None):
    m, k = x.shape
    n = w_stack.shape[2]
    tm = min(tm, m)
    assert m % tm == 0
    in_specs = [
        pl.BlockSpec((tm, k), lambda i: (i, 0)),
        pl.BlockSpec((None, k, n), lambda i: (layer, 0, 0), pipeline_mode=pl.Buffered(1)),
        pl.BlockSpec((tm, n), lambda i: (i, 0)),
    ]
    args = [x, w_stack, resid]
    if out_gain is not None:
        in_specs.append(pl.BlockSpec((1, n), lambda i: (0, 0)))
        args.append(out_gain)
    return pl.pallas_call(
        _matmul_resid_kernel,
        grid=(m // tm,),
        in_specs=in_specs,
        out_specs=pl.BlockSpec((tm, n), lambda i: (i, 0)),
        out_shape=jax.ShapeDtypeStruct((m, n), F32),
        compiler_params=_cparams("parallel"),
        name=name,
    )(*args)


def _qkv_b_kernel(x_ref, xnext_ref, g_ref, w_ref, o_ref, xa_ref, xb_ref, *lane_chunks, dil):
    tm, d = xnext_ref.shape
    n = tm // dil
    gain = g_ref[...]

    def prepare(src_ref, row0, dst_ref):
        if dil == 1:
            dst_ref[...] = _rms(src_ref[row0:row0 + tm, :], gain).astype(BF16)
            return
        xc_ref, = lane_chunks
        for k in range(d // LANES):
            xc_ref[k] = src_ref[row0:row0 + tm, k * LANES:(k + 1) * LANES]
        for c in range(dil):
            rows = jnp.concatenate(
                [xc_ref[k, pl.ds(c, n, stride=dil), :] for k in range(d // LANES)], axis=1)
            dst_ref[c * n:(c + 1) * n, :] = _rms(rows, gain).astype(BF16)

    def project(src_ref, half):
        y = jnp.dot(src_ref[...], w_ref[...], preferred_element_type=F32)
        for c in range(dil):
            o_ref[0, c, half * n:(half + 1) * n, :] = y[c * n:(c + 1) * n].astype(BF16)

    @pl.when(pl.program_id(0) == 0)
    def _():
        prepare(x_ref, 0, xa_ref)

    project(xa_ref, 0)
    prepare(x_ref, tm, xb_ref)
    project(xb_ref, 1)
    prepare(xnext_ref, 0, xa_ref)


def _qkv_b_group(h, gain, w_stack, layer, g, dil, batch, seq, tm):
    m, d = h.shape
    gw = 3 * B_GROUP_WIDTH
    tm = min(tm, seq // 2)
    assert seq % (2 * tm) == 0 and tm % (dil * BF16_SUBLANES) == 0
    steps_per_seq = seq // (2 * tm)
    n = tm // dil
    last_tile = m // tm - 1
    return pl.pallas_call(
        functools.partial(_qkv_b_kernel, dil=dil),
        grid=(m // (2 * tm),),
        in_specs=[
            pl.BlockSpec((2 * tm, d), lambda i: (i, 0)),
            pl.BlockSpec((tm, d), lambda i: (jnp.minimum(2 * i + 2, last_tile), 0)),
            pl.BlockSpec((1, d), lambda i: (0, 0)),
            pl.BlockSpec((None, d, gw), lambda i: (layer, 0, g), pipeline_mode=pl.Buffered(1)),
        ],
        out_specs=pl.BlockSpec((1, dil, 2 * n, gw), lambda i: (i // steps_per_seq, 0, i % steps_per_seq, 0)),
        out_shape=jax.ShapeDtypeStruct((batch, dil, seq // dil, gw), BF16),
        scratch_shapes=[pltpu.VMEM((tm, d), BF16), pltpu.VMEM((tm, d), BF16)] + (
            [pltpu.VMEM((d // LANES, tm, LANES), F32)] if dil > 1 else []),
        compiler_params=_cparams("arbitrary"),
        name=f"qkv_b_g{g}",
    )(h, h, gain, w_stack)


def _t5_bucket(rel):
    nb = REL_BUCKETS // 2
    max_exact = nb // 2
    base = jnp.where(rel > 0, nb, 0)
    n = jnp.abs(rel)
    nf = jnp.maximum(n, 1).astype(F32)
    large = max_exact + (jnp.log(nf / max_exact) / math.log(REL_MAX_DISTANCE / max_exact)
                         * (nb - max_exact)).astype(jnp.int32)
    large = jnp.minimum(large, nb - 1)
    return base + jnp.where(n < max_exact, n, large)


def _b_bias_tables(rel_bias, g, dil, tq):
    w = tq + 2 * B_HALO
    kk = jnp.arange(w)
    rel = kk[None, :] - B_HALO - jnp.arange(tq)[:, None]
    hg = B_HEADS_PER_GROUP
    bucket = _t5_bucket(rel * dil)
    table = rel_bias[:, g * hg:(g + 1) * hg].astype(F32)
    bias = sum(jnp.where(bucket[None] == b, table[b][:, None, None], 0.0) for b in range(REL_BUCKETS))
    bias = jnp.where((jnp.abs(rel) <= B_HALO)[None], bias, NEG_INF)
    edges = []
    for first, last in ((False, False), (True, False), (False, True), (True, True)):
        off = jnp.zeros((w,), bool)
        if first:
            off = off | (kk < B_HALO)
        if last:
            off = off | (kk >= tq + B_HALO)
        edges.append(jnp.where(off, NEG_INF, 0.0).astype(F32)[None, :])
    return bias, jnp.stack(edges)


def _attn_b_kernel(q_ref, kp_ref, kc_ref, kn_ref, vp_ref, vc_ref, vn_ref, bias_ref, edge_ref, o_ref, lz_ref):
    tq = q_ref.shape[2]
    scale = HEAD_DIM ** -0.5
    lane_head = lax.broadcasted_iota(jnp.int32, (1, LANES), 1) // (LANES // B_HEADS_PER_GROUP)
    lz_all = jnp.zeros((tq, LANES), F32)
    edge = edge_ref[0]
    for hh in range(B_HEADS_PER_GROUP):
        cs = slice(hh * HEAD_DIM, (hh + 1) * HEAD_DIM)
        q = q_ref[0, 0, :, cs]
        kwin = jnp.concatenate([kp_ref[0, 0, :, cs], kc_ref[0, 0, :, cs], kn_ref[0, 0, :, cs]], axis=0)
        vwin = jnp.concatenate([vp_ref[0, 0, :, cs], vc_ref[0, 0, :, cs], vn_ref[0, 0, :, cs]], axis=0)
        s = lax.dot_general(q, kwin, (((1,), (1,)), ((), ())), preferred_element_type=F32)
        s = s * scale + (bias_ref[hh] + edge)
        m = jnp.max(s, axis=-1, keepdims=True)
        p = jnp.exp(s - m)
        l = jnp.sum(p, axis=-1, keepdims=True)
        o = jnp.dot(p.astype(BF16), vwin, preferred_element_type=F32)
        o_ref[0, 0, :, cs] = o / l
        lz_all = jnp.where(lane_head == hh, m + jnp.log(l), lz_all)
    lz_ref[0, 0] = lz_all


def _attn_b_group(qkv, rel_bias, g, dil):
    batch, _, ln, _ = qkv.shape
    tq = min(B_TQ, ln)
    assert ln % tq == 0 and tq % B_HALO == 0
    n_tiles = ln // tq
    hpt = tq // B_HALO
    halo_blocks = ln // B_HALO
    w = tq + 2 * B_HALO
    bias, edges = _b_bias_tables(rel_bias, g, dil, tq)

    def cur(which):
        return pl.BlockSpec((1, 1, tq, B_GROUP_WIDTH), lambda b, c, i: (b, c, i, which))

    def prev(which):
        return pl.BlockSpec((1, 1, B_HALO, B_GROUP_WIDTH),
                            lambda b, c, i: (b, c, jnp.maximum(i * hpt - 1, 0), which))

    def nxt(which):
        return pl.BlockSpec((1, 1, B_HALO, B_GROUP_WIDTH),
                            lambda b, c, i: (b, c, jnp.minimum((i + 1) * hpt, halo_blocks - 1), which))

    def edge_row(i):
        return (i == 0).astype(jnp.int32) + 2 * (i == n_tiles - 1).astype(jnp.int32)

    return pl.pallas_call(
        _attn_b_kernel,
        grid=(batch, dil, n_tiles),
        in_specs=[
            cur(0), prev(1), cur(1), nxt(1), prev(2), cur(2), nxt(2),
            pl.BlockSpec((B_HEADS_PER_GROUP, tq, w), lambda b, c, i: (0, 0, 0)),
            pl.BlockSpec((1, 1, w), lambda b, c, i: (edge_row(i), 0, 0)),
        ],
        out_specs=[
            pl.BlockSpec((1, 1, tq, B_GROUP_WIDTH), lambda b, c, i: (b, c, i, 0)),
            pl.BlockSpec((1, 1, tq, LANES), lambda b, c, i: (b, c, i, 0)),
        ],
        out_shape=[
            jax.ShapeDtypeStruct((batch, dil, ln, B_GROUP_WIDTH), F32),
            jax.ShapeDtypeStruct((batch, dil, ln, LANES), F32),
        ],
        compiler_params=_cparams("parallel", "parallel", "parallel"),
        name=f"attn_b_g{g}",
    )(qkv, qkv, qkv, qkv, qkv, qkv, qkv, bias, edges)


def _mix_proj_kernel(o0_ref, o1_ref, o2_ref, z0_ref, z1_ref, z2_ref, w_ref, r_ref, out_ref,
                     y_ref, ot_ref, zt_ref):
    tm = out_ref.shape[0]

    def to_token_order(src_ref, dst_ref, dil):
        rows = tm // dil
        for c in range(dil):
            for k in range(dst_ref.shape[0]):
                dst_ref[k, pl.ds(c, rows, stride=dil), :] = src_ref[0, c, :, k * LANES:(k + 1) * LANES]

    z = []
    for g, z_ref in enumerate((z0_ref, z1_ref, z2_ref)):
        dil = B_GROUPS[g][1]
        if dil == 1:
            z.append(z_ref[0, 0])
        else:
            to_token_order(z_ref, zt_ref.at[g:g + 1], dil)
            z.append(zt_ref[g])
    zmax = functools.reduce(jnp.maximum, z)
    e = [jnp.exp(zz - zmax) for zz in z]
    den = functools.reduce(lambda a, b: a + b, e)
    lanes_per_head = LANES // B_HEADS_PER_GROUP
    for g, o_ref in enumerate((o0_ref, o1_ref, o2_ref)):
        dil = B_GROUPS[g][1]
        wt = e[g] / den
        if dil > 1:
            to_token_order(o_ref, ot_ref, dil)
        for hh in range(B_HEADS_PER_GROUP):
            wcol = wt[:, hh * lanes_per_head:hh * lanes_per_head + 1]
            cs = slice(hh * HEAD_DIM, (hh + 1) * HEAD_DIM)
            og = o_ref[0, 0, :, cs] if dil == 1 else ot_ref[hh]
            y_ref[:, g * B_GROUP_WIDTH + hh * HEAD_DIM:g * B_GROUP_WIDTH + (hh + 1) * HEAD_DIM] = (
                wcol * og).astype(BF16)

    out_ref[...] = r_ref[...] + jnp.dot(y_ref[...], w_ref[...], preferred_element_type=F32)


def _mix_proj(outs, lzs, w_stack, layer, resid, seq, tm):
    m = resid.shape[0]
    _, k, n = w_stack.shape
    tm = min(tm, seq)
    assert seq % tm == 0
    tiles_per_seq = seq // tm

    def sub_major(width, dil):
        assert tm % (dil * 8) == 0
        return pl.BlockSpec((1, dil, tm // dil, width),
                            lambda i: (i // tiles_per_seq, 0, i % tiles_per_seq, 0))

    dils = [dil for _, dil in B_GROUPS]
    return pl.pallas_call(
        _mix_proj_kernel,
        grid=(m // tm,),
        in_specs=[sub_major(B_GROUP_WIDTH, dil) for dil in dils] + [sub_major(LANES, dil) for dil in dils] + [
            pl.BlockSpec((None, k, n), lambda i: (layer, 0, 0), pipeline_mode=pl.Buffered(1)),
            pl.BlockSpec((tm, n), lambda i: (i, 0))],
        out_specs=pl.BlockSpec((tm, n), lambda i: (i, 0)),
        out_shape=jax.ShapeDtypeStruct((m, n), F32),
        scratch_shapes=[pltpu.VMEM((tm, k), BF16),
                        pltpu.VMEM((B_HEADS_PER_GROUP, tm, HEAD_DIM), F32),
                        pltpu.VMEM((len(dils), tm, LANES), F32)],
        compiler_params=_cparams("parallel"),
        name="mix_proj_b",
    )(*outs, *lzs, w_stack, resid)


def _ffn_up_kernel(xp_ref, x_ref, xn_ref, g_ref, wg_ref, wv_ref, cwg_ref, cwv_ref, cbg_ref, cbv_ref,
                   o_ref, xs_ref, *, seq):
    i = pl.program_id(0)
    tm = x_ref.shape[0]
    hl = CONV_HALO
    ext = tm + 2 * hl

    @pl.when(pl.program_id(1) == 0)
    def _():
        gain = g_ref[...]
        at_start = (i * tm) % seq == 0
        at_end = ((i + 1) * tm) % seq == 0
        xs_ref[0:hl, :] = jnp.where(at_start, 0.0, _rms(xp_ref[...], gain)).astype(BF16)
        xs_ref[hl:hl + tm, :] = _rms(x_ref[...], gain).astype(BF16)
        xs_ref[hl + tm:ext, :] = jnp.where(at_end, 0.0, _rms(xn_ref[...], gain)).astype(BF16)

    xs = xs_ref[...]

    def conv(w_ref, cw_ref, cb_ref):
        u = jnp.dot(xs, w_ref[...], preferred_element_type=F32)
        below = pltpu.roll(u, 1, 0)[hl:hl + tm]
        above = pltpu.roll(u, ext - 1, 0)[hl:hl + tm]
        cw = cw_ref[...]
        return below * cw[0:1] + u[hl:hl + tm] * cw[1:2] + above * cw[2:3] + cb_ref[...]

    gate = conv(wg_ref, cwg_ref, cbg_ref)
    val = conv(wv_ref, cwv_ref, cbv_ref)
    o_ref[...] = (gate * jax.nn.sigmoid(gate) * val).astype(BF16)


def _ffn_up(h, gain, w_up_stack, layer, conv_w, conv_b, seq, tm, tn):
    m, d = h.shape
    dff = w_up_stack.shape[2] // 2
    tm, tn = min(tm, seq), min(tn, dff)
    hl = CONV_HALO
    assert m % tm == 0 and seq % tm == 0 and dff % tn == 0 and tm % hl == 0
    nj = dff // tn
    rpt = tm // hl
    last_halo = m // hl - 1
    return pl.pallas_call(
        functools.partial(_ffn_up_kernel, seq=seq),
        grid=(m // tm, nj),
        in_specs=[
            pl.BlockSpec((hl, d), lambda i, j: (jnp.maximum(i * rpt - 1, 0), 0)),
            pl.BlockSpec((tm, d), lambda i, j: (i, 0)),
            pl.BlockSpec((hl, d), lambda i, j: (jnp.minimum((i + 1) * rpt, last_halo), 0)),
            pl.BlockSpec((1, d), lambda i, j: (0, 0)),
            pl.BlockSpec((None, d, tn), lambda i, j: (layer, 0, j)),
            pl.BlockSpec((None, d, tn), lambda i, j: (layer, 0, nj + j)),
            pl.BlockSpec((3, tn), lambda i, j: (0, j)),
            pl.BlockSpec((3, tn), lambda i, j: (0, nj + j)),
            pl.BlockSpec((1, tn), lambda i, j: (0, j)),
            pl.BlockSpec((1, tn), lambda i, j: (0, nj + j)),
        ],
        out_specs=pl.BlockSpec((tm, tn), lambda i, j: (i, j)),
        out_shape=jax.ShapeDtypeStruct((m, dff), BF16),
        scratch_shapes=[pltpu.VMEM((tm + 2 * hl, d), BF16)],
        compiler_params=_cparams("parallel", "arbitrary"),
        name="ffn_up",
    )(h, h, h, gain, w_up_stack, w_up_stack, conv_w, conv_w, conv_b, conv_b)


def _rope_tables(seq):
    rows = seq // GRID_W
    row_ids = jnp.repeat(jnp.arange(rows, dtype=F32), GRID_W)
    col_ids = jnp.tile(jnp.arange(GRID_W, dtype=F32), rows)
    quarter = HEAD_DIM // 4
    inv_freq = ROPE_THETA ** (-jnp.arange(quarter, dtype=F32) / quarter)
    ang_r = row_ids[:, None] * inv_freq[None, :]
    ang_c = col_ids[:, None] * inv_freq[None, :]
    cos_t = jnp.concatenate([jnp.cos(ang_r), jnp.cos(ang_r), jnp.cos(ang_c), jnp.cos(ang_c)], axis=-1)
    sin_t = jnp.concatenate([-jnp.sin(ang_r), jnp.sin(ang_r), -jnp.sin(ang_c), jnp.sin(ang_c)], axis=-1)
    return cos_t, sin_t


def kernel(x, a_w_qkv, a_w_o, a_q_gain, a_k_gain, b_w_qkv, b_w_o, rel_bias, mix_norm, ffn_norm, w_up, conv_w,
           conv_b, w_down, final_norm):
    batch, seq, d = x.shape
    m = batch * seq
    depth = mix_norm.shape[0]
    h = x.reshape(m, d)
    cos_t, sin_t = _rope_tables(seq)
    a_w_qkv, a_w_o, b_w_qkv, b_w_o, w_up, w_down = (
        w.astype(BF16) for w in (a_w_qkv, a_w_o, b_w_qkv, b_w_o, w_up, w_down))
    for i in range(depth):
        jj = i // 2
        gain = mix_norm[i][None, :]
        if i % 2 == 0:
            qkv = _qkv_a(h, gain, a_w_qkv, jj, cos_t, sin_t, a_q_gain[jj][None, :], a_k_gain[jj][None, :], seq)
            o = _attn_a(qkv, batch, seq)
            h = _matmul_resid(o, a_w_o, jj, h, TM // 2, "out_proj_a")
        else:
            outs, lzs = [], []
            for g, (_, dil) in enumerate(B_GROUPS):
                qkv = _qkv_b_group(h, gain, b_w_qkv, jj, g, dil, batch, seq, TM // 4)
                o, lz = _attn_b_group(qkv, rel_bias, g, dil)
                outs.append(o)
                lzs.append(lz)
            h = _mix_proj(outs, lzs, b_w_o, jj, h, seq, TM // 4)
        act = _ffn_up(h, ffn_norm[i][None, :], w_up, i, conv_w[i], conv_b[i][None, :], seq, TM, TN)
        last = i == depth - 1
        h = _matmul_resid(act, w_down, i, h, TM // 4, "ffn_down_norm" if last else "ffn_down",
                          out_gain=final_norm[None, :] if last else None)
    return h.reshape(batch, seq, d)
```
